```python
import math, functools
import jax, jax.numpy as jnp
from jax import lax
import numpy as np

D_MODEL = 1024
BATCH = 2
SEQ = 8192
DEPTH = 1
DEC_BATCH = 32
DEC_SEQ = 8
PAST_LEN = 16384
PAGE_SIZE = 128

N_ATT_HEADS = 4
ATT_QK_DIM = 64
ATT_V_DIM = 2 * ATT_QK_DIM
ATT_WIDTH = N_ATT_HEADS * ATT_V_DIM
CONV_CH = D_MODEL - ATT_WIDTH
CONV_K = 31
Q_W = N_ATT_HEADS * 2 * ATT_QK_DIM
IN_W = 2 * Q_W + ATT_WIDTH + 2 * CONV_CH
N_GROUPS = 4
EXPERTS_PER_GROUP = 4
N_EXPERTS = N_GROUPS * EXPERTS_PER_GROUP
TOP_K_IN_GROUP = 2
D_EXPERT = D_MODEL // 4
Q_BLOCK = 128
EPS = 1e-6
ATT_SCALE = 1.0 / math.sqrt(ATT_QK_DIM)

kernel_name = "hymba_diffattn_conformer_hmoe_step"

F32 = jnp.float32


def _rms(x, g):
    xf = x.astype(F32)
    y = xf * lax.rsqrt(jnp.mean(xf * xf, axis=-1, keepdims=True) + EPS)
    return (y * g.astype(F32)).astype(x.dtype)


def _layernorm(x, g, b):
    xf = x.astype(F32)
    mu = jnp.mean(xf, axis=-1, keepdims=True)
    var = jnp.mean(jnp.square(xf - mu), axis=-1, keepdims=True)
    y = (xf - mu) * lax.rsqrt(var + EPS) * g.astype(F32) + b.astype(F32)
    return y.astype(x.dtype)


def _alibi_slopes():
    return 2.0 ** (-8.0 * jnp.arange(1, N_ATT_HEADS + 1, dtype=F32) / N_ATT_HEADS)


def _adaln(c, w_ada, b_ada):
    mod = jax.nn.silu(c) @ w_ada + b_ada
    return mod.reshape(c.shape[0], 6, 1, D_MODEL)


def _mixer_inputs(h, w_in, g_q, g_k):
    B, T, _ = h.shape
    z = h @ w_in
    q = z[..., :Q_W].reshape(B, T, N_ATT_HEADS, 2, ATT_QK_DIM)
    k = z[..., Q_W:2 * Q_W].reshape(B, T, N_ATT_HEADS, 2, ATT_QK_DIM)
    v = z[..., 2 * Q_W:2 * Q_W + ATT_WIDTH].reshape(B, T, N_ATT_HEADS, ATT_V_DIM)
    a = z[..., 2 * Q_W + ATT_WIDTH:]
    u = a[..., :CONV_CH] * jax.nn.sigmoid(a[..., CONV_CH:])
    return _rms(q, g_q), _rms(k, g_k), v, u


def _diff_attn_prompt(q, k, v, lam):
    B, S = q.shape[:2]
    nb = S // Q_BLOCK
    slopes = _alibi_slopes()[None, :, None, None, None]
    kf = k.astype(F32)
    vf = v.astype(F32)
    kpos = jnp.arange(S)
    qb = q.astype(F32).reshape(B, nb, Q_BLOCK, N_ATT_HEADS, 2, ATT_QK_DIM).swapaxes(0, 1)

    def one_block(args):
        qi, i = args
        qpos = i * Q_BLOCK + jnp.arange(Q_BLOCK)
        dist = qpos[:, None] - kpos[None, :]
        s = jnp.einsum('bthmd,bshmd->bhmts', qi, kf) * ATT_SCALE - slopes * dist.astype(F32)
        s = jnp.where(dist >= 0, s, -jnp.inf)
        p = jax.nn.softmax(s, axis=-1)
        o = jnp.einsum('bhmts,bshe->bthme', p, vf)
        return o[..., 0, :] - lam * o[..., 1, :]

    out = lax.map(one_block, (qb, jnp.arange(nb)))
    return out.swapaxes(0, 1).reshape(B, S, N_ATT_HEADS, ATT_V_DIM)


def _online_update(carry, s, vblk):
    m, l, acc = carry
    m_new = jnp.maximum(m, jnp.max(s, axis=-1))
    alpha = jnp.exp(m - m_new)
    p = jnp.exp(s - m_new[..., None])
    l = l * alpha + jnp.sum(p, axis=-1)
    acc = acc * alpha[..., None] + jnp.einsum('bhmts,bshe->bhmte', p, vblk)
    return (m_new, l, acc)


def _diff_attn_sample(q, k, v, cache_k_l, cache_v_l, page_table, lam):
    DB, T = q.shape[:2]
    n_pages = page_table.shape[1]
    past = n_pages * PAGE_SIZE
    slopes = _alibi_slopes()[None, :, None, None, None]
    qf = q.astype(F32)
    qpos = past + jnp.arange(T)
    init = (jnp.full((DB, N_ATT_HEADS, 2, T), -jnp.inf, F32),
            jnp.zeros((DB, N_ATT_HEADS, 2, T), F32),
            jnp.zeros((DB, N_ATT_HEADS, 2, T, ATT_V_DIM), F32))

    def page_step(carry, j):
        phys = page_table[:, j]
        kj = cache_k_l[phys].astype(F32)
        vj = cache_v_l[phys].astype(F32)
        kpos = j * PAGE_SIZE + jnp.arange(PAGE_SIZE)
        dist = (qpos[:, None] - kpos[None, :]).astype(F32)
        s = jnp.einsum('bthmd,bshmd->bhmts', qf, kj) * ATT_SCALE - slopes * dist
        return _online_update(carry, s, vj), None

    carry, _ = lax.scan(page_step, init, jnp.arange(n_pages))
    dist = qpos[:, None] - qpos[None, :]
    s = jnp.einsum('bthmd,bshmd->bhmts', qf, k.astype(F32)) * ATT_SCALE - slopes * dist.astype(F32)
    s = jnp.where(dist >= 0, s, -jnp.inf)
    m, l, acc = _online_update(carry, s, v.astype(F32))
    o = acc / l[..., None]
    out = o[:, :, 0] - lam * o[:, :, 1]
    return out.transpose(0, 2, 1, 3)


def _conv_branch(u, prefix, w_dw, b_dw, g_ln, b_ln, w_pw):
    seq = jnp.concatenate([prefix.astype(u.dtype), u], axis=1)
    y = lax.conv_general_dilated(seq, w_dw[:, None, :].astype(seq.dtype), (1,), 'VALID',
                                 dimension_numbers=('NWC', 'WIO', 'NWC'),
                                 feature_group_count=CONV_CH) + b_dw
    y = _layernorm(y, g_ln, b_ln)
    y = jax.nn.silu(y) @ w_pw
    return y, seq[:, -(CONV_K - 1):]


def _merge(att, conv, g_att, lam_init, w_out):
    B, T = conv.shape[:2]
    att = _rms(att, g_att) * (1.0 - lam_init)
    mix = jnp.concatenate([att.reshape(B, T, ATT_WIDTH).astype(conv.dtype), conv], axis=-1)
    return mix @ w_out


def _hier_moe(h, w_rg, b_rg, w_re, b_re, w_g, w_u, w_d):
    B, T, D = h.shape
    hf = h.reshape(-1, D)
    n = hf.shape[0]
    g_logits = jnp.dot(hf, w_rg, preferred_element_type=F32) + b_rg.astype(F32)
    g_prob = jax.nn.softmax(g_logits, axis=-1)
    g_val, g_idx = lax.top_k(g_prob, 1)
    e_logits = (jnp.dot(hf, w_re, preferred_element_type=F32) + b_re.astype(F32)).reshape(n, N_GROUPS, EXPERTS_PER_GROUP)
    e_in_group = jnp.einsum('nge,ng->ne', e_logits, jax.nn.one_hot(g_idx[:, 0], N_GROUPS, dtype=F32))
    e_val, e_idx = lax.top_k(e_in_group, TOP_K_IN_GROUP)
    e_w = jax.nn.softmax(e_val, axis=-1) * g_val
    e_glob = g_idx * EXPERTS_PER_GROUP + e_idx
    combine = jnp.sum(jax.nn.one_hot(e_glob, N_EXPERTS, dtype=F32) * e_w[..., None], axis=1)
    y = jnp.zeros((n, D), F32)
    for e in range(N_EXPERTS):
        hid = jax.nn.silu(hf @ w_g[e]) * (hf @ w_u[e])
        y = y + combine[:, e:e + 1] * jnp.dot(hid, w_d[e], preferred_element_type=F32)
    return y.reshape(B, T, D).astype(h.dtype)


def _layer(x, c, conv_prefix, attend, lam_init, w_ada, b_ada, g_norm_mix, w_in, g_q, g_k,
           g_attn_out, w_dw, b_dw, g_conv_ln, b_conv_ln, w_conv_pw, w_out, g_norm_ffn,
           w_rg, b_rg, w_re, b_re, w_g, w_u, w_d):
    mod = _adaln(c, w_ada, b_ada)
    sh1, sc1, gt1, sh2, sc2, gt2 = mod[:, 0], mod[:, 1], mod[:, 2], mod[:, 3], mod[:, 4], mod[:, 5]
    h = _rms(x, g_norm_mix) * (1.0 + sc1) + sh1
    q, k, v, u = _mixer_inputs(h, w_in, g_q, g_k)
    att = attend(q, k, v)
    conv, conv_state = _conv_branch(u, conv_prefix, w_dw, b_dw, g_conv_ln, b_conv_ln, w_conv_pw)
    x = x + gt1 * _merge(att, conv, g_attn_out, lam_init, w_out)
    h2 = _rms(x, g_norm_ffn) * (1.0 + sc2) + sh2
    x = x + gt2 * _hier_moe(h2, w_rg, b_rg, w_re, b_re, w_g, w_u, w_d)
    return x, k, v, conv_state


def setup_inputs(seed: int = 0) -> dict:
    key = jax.random.key(seed)
    keys = iter(jax.random.split(key, 48))

    def nrm(shape, scale):
        return jax.random.normal(next(keys), shape, F32) * scale

    def gain(shape):
        return 1.0 + nrm(shape, 0.05)

    n_pages = PAST_LEN // PAGE_SIZE
    n_used = DEC_BATCH * n_pages
    n_phys = n_used + max(1, n_used // 4)
    page_table = jax.random.permutation(next(keys), n_phys)[:n_used].reshape(DEC_BATCH, n_pages).astype(jnp.int32)
    L = DEPTH
    return {
        "x_prompt": nrm((BATCH, SEQ, D_MODEL), 1.0),
        "x_sample": nrm((DEC_BATCH, DEC_SEQ, D_MODEL), 1.0),
        "c_prompt": nrm((BATCH, D_MODEL), 1.0),
        "c_sample": nrm((DEC_BATCH, D_MODEL), 1.0),
        "cache_k": nrm((L, n_phys, PAGE_SIZE, N_ATT_HEADS, 2, ATT_QK_DIM), 1.0),
        "cache_v": nrm((L, n_phys, PAGE_SIZE, N_ATT_HEADS, ATT_V_DIM), 1.0),
        "state_conv": nrm((L, DEC_BATCH, CONV_K - 1, CONV_CH), 0.5),
        "page_table": page_table,
        "w_ada": nrm((L, D_MODEL, 6 * D_MODEL), 0.5 * D_MODEL ** -0.5),
        "b_ada": nrm((L, 6 * D_MODEL), 0.02),
        "g_norm_mix": gain((L, D_MODEL)),
        "w_in": nrm((L, D_MODEL, IN_W), D_MODEL ** -0.5),
        "g_q": gain((L, ATT_QK_DIM)),
        "g_k": gain((L, ATT_QK_DIM)),
        "lambda_q1": nrm((L, ATT_QK_DIM), 0.1),
        "lambda_k1": nrm((L, ATT_QK_DIM), 0.1),
        "lambda_q2": nrm((L, ATT_QK_DIM), 0.1),
        "lambda_k2": nrm((L, ATT_QK_DIM), 0.1),
        "g_attn_out": gain((L, ATT_V_DIM)),
        "w_dw": nrm((L, CONV_K, CONV_CH), CONV_K ** -0.5),
        "b_dw": nrm((L, CONV_CH), 0.02),
        "g_conv_ln": gain((L, CONV_CH)),
        "b_conv_ln": nrm((L, CONV_CH), 0.02),
        "w_conv_pw": nrm((L, CONV_CH, CONV_CH), CONV_CH ** -0.5),
        "w_out": nrm((L, D_MODEL, D_MODEL), D_MODEL ** -0.5),
        "g_norm_ffn": gain((L, D_MODEL)),
        "w_router_group": nrm((L, D_MODEL, N_GROUPS), D_MODEL ** -0.5),
        "b_router_group": nrm((L, N_GROUPS), 0.01),
        "w_router_expert": nrm((L, D_MODEL, N_EXPERTS), D_MODEL ** -0.5),
        "b_router_expert": nrm((L, N_EXPERTS), 0.01),
        "w_exp_gate": nrm((L, N_EXPERTS, D_MODEL, D_EXPERT), D_MODEL ** -0.5),
        "w_exp_up": nrm((L, N_EXPERTS, D_MODEL, D_EXPERT), D_MODEL ** -0.5),
        "w_exp_down": nrm((L, N_EXPERTS, D_EXPERT, D_MODEL), D_EXPERT ** -0.5),
    }


def reference(x_prompt, x_sample, c_prompt, c_sample, cache_k, cache_v, state_conv, page_table,
              w_ada, b_ada, g_norm_mix, w_in, g_q, g_k, lambda_q1, lambda_k1, lambda_q2, lambda_k2,
              g_attn_out, w_dw, b_dw, g_conv_ln, b_conv_ln, w_conv_pw, w_out, g_norm_ffn,
              w_router_group, b_router_group, w_router_expert, b_router_expert,
              w_exp_gate, w_exp_up, w_exp_down):
    yp, ys = x_prompt, x_sample
    kp_l, vp_l, cp_l, ks_l, vs_l, cs_l = [], [], [], [], [], []
    for l in range(DEPTH):
        lam_init = 0.8 - 0.6 * math.exp(-0.3 * l)
        lam = (jnp.exp(jnp.sum(lambda_q1[l].astype(F32) * lambda_k1[l].astype(F32)))
               - jnp.exp(jnp.sum(lambda_q2[l].astype(F32) * lambda_k2[l].astype(F32))) + lam_init)
        shared = (w_ada[l], b_ada[l], g_norm_mix[l], w_in[l], g_q[l], g_k[l], g_attn_out[l],
                  w_dw[l], b_dw[l], g_conv_ln[l], b_conv_ln[l], w_conv_pw[l], w_out[l], g_norm_ffn[l],
                  w_router_group[l], b_router_group[l], w_router_expert[l], b_router_expert[l],
                  w_exp_gate[l], w_exp_up[l], w_exp_down[l])
        prompt_prefix = jnp.zeros((yp.shape[0], CONV_K - 1, CONV_CH), yp.dtype)
        yp, kp, vp, cp = _layer(yp, c_prompt, prompt_prefix,
                                functools.partial(_diff_attn_prompt, lam=lam), lam_init, *shared)
        ys, ks, vs, cs = _layer(ys, c_sample, state_conv[l],
                                functools.partial(_diff_attn_sample, cache_k_l=cache_k[l],
                                                  cache_v_l=cache_v[l], page_table=page_table, lam=lam),
                                lam_init, *shared)
        kp_l.append(kp); vp_l.append(vp); cp_l.append(cp)
        ks_l.append(ks); vs_l.append(vs); cs_l.append(cs)
    new_k_prompt = jnp.stack(kp_l)
    new_v_prompt = jnp.stack(vp_l)
    new_conv_prompt = jnp.stack(cp_l)
    new_k_sample = jnp.stack(ks_l)
    new_v_sample = jnp.stack(vs_l)
    new_conv_sample = jnp.stack(cs_l)
    return (yp, ys, new_k_prompt, new_v_prompt, new_conv_prompt, new_k_sample, new_v_sample, new_conv_sample)
```

```python
import functools
import math

import jax
import jax.numpy as jnp
from jax import lax
from jax.experimental import pallas as pl
from jax.experimental.pallas import tpu as pltpu

F32 = jnp.float32
BF16 = jnp.bfloat16

N_HEADS = 4
QK_DIM = 64
V_DIM = 128
ATT_W = N_HEADS * V_DIM
Q_W = N_HEADS * 2 * QK_DIM
CONV_K = 31
N_GROUPS = 4
EXPERTS_PER_GROUP = 4
N_EXPERTS = N_GROUPS * EXPERTS_PER_GROUP
EPS = 1e-6
ATT_SCALE = 1.0 / math.sqrt(QK_DIM)
PAGE = 128
NEG = -1e30

LANE = 128
MXU = 256
HALO = 32
VMEM_LIMIT = 56 * 1024 * 1024


def _cparams(sem):
    return pltpu.CompilerParams(dimension_semantics=sem, vmem_limit_bytes=VMEM_LIMIT)


def _dot(a, b):
    return jnp.dot(a, b, preferred_element_type=F32)


def _dot_nt(a, b):
    return lax.dot_general(a, b, (((1,), (1,)), ((), ())), preferred_element_type=F32)


def _split_bf16(x):
    hi = x.astype(BF16)
    lo = (x - hi.astype(F32)).astype(BF16)
    return hi, lo


def _adaln_body(c_ref, w_ref, b_ref, o_ref):
    c = c_ref[...]
    s = c * (1.0 / (1.0 + jnp.exp(-c)))
    o_ref[...] = jnp.dot(s, w_ref[...], preferred_element_type=F32,
                         precision=lax.Precision.HIGHEST) + b_ref[...]


def _adaln(c, w_ada, b_ada):
    nb, d = c.shape
    n_out = w_ada.shape[1]
    bn = 1024
    return pl.pallas_call(
        _adaln_body,
        grid=(n_out // bn,),
        in_specs=[pl.BlockSpec((nb, d), lambda j: (0, 0)),
                  pl.BlockSpec((d, bn), lambda j: (0, j)),
                  pl.BlockSpec((1, bn), lambda j: (0, j))],
        out_specs=pl.BlockSpec((nb, bn), lambda j: (0, j)),
        out_shape=jax.ShapeDtypeStruct((nb, n_out), F32),
        compiler_params=_cparams(("arbitrary",)),
        name="adaln",
    )(c, w_ada, b_ada.reshape(1, n_out))


def _proj_body(x_ref, sh_ref, sc_ref, gmix_ref, w_ref, gq_ref, gk_ref, pm_ref, *out_refs, prompt):
    x = x_ref[...]
    h = x * lax.rsqrt(jnp.mean(x * x, axis=-1, keepdims=True) + EPS) * gmix_ref[...]
    h = h * (1.0 + sc_ref[0]) + sh_ref[0]
    hb = h.astype(BF16)
    pm = pm_ref[...]

    def chunk_rms(z, g):
        outs = []
        for c in range(Q_W // MXU):
            zc = z[:, c * MXU:(c + 1) * MXU]
            hi, lo = _split_bf16(zc * zc)
            ms = _dot(hi, pm) + _dot(lo, pm)
            outs.append(zc * lax.rsqrt(ms + EPS))
        return jnp.concatenate(outs, axis=-1) * g

    qn = chunk_rms(_dot(hb, w_ref[:, 0:Q_W]), gq_ref[...])
    kn = chunk_rms(_dot(hb, w_ref[:, Q_W:2 * Q_W]), gk_ref[...])
    v = _dot(hb, w_ref[:, 2 * Q_W:2 * Q_W + ATT_W])
    a = _dot(hb, w_ref[:, 2 * Q_W + ATT_W:])
    cw = a.shape[-1] // 2
    a2 = a[:, cw:]
    u = a[:, :cw] * (1.0 / (1.0 + jnp.exp(-a2)))
    if prompt:
        k_ref, v_ref, u_ref, qs_ref, kb_ref, vt_ref = out_refs
        qs_ref[...] = (qn * ATT_SCALE).astype(BF16)
        kb_ref[...] = kn.astype(BF16)
        vt_ref[0, 0] = v.T.astype(BF16)
    else:
        q_ref, k_ref, v_ref, u_ref = out_refs
        q_ref[...] = qn
    k_ref[...] = kn
    v_ref[...] = v
    u_ref[...] = u


def _proj(x2, sh, sc, gmix, w_in_b, gq_t, gk_t, pm, *, tm, tiles_per_seq, prompt):
    n, d = x2.shape
    nt = n // tm
    in_w = w_in_b.shape[1]
    conv_ch = (in_w - 2 * Q_W - ATT_W) // 2
    mod_rows = sh.shape[1]
    row = lambda i: (i, 0)
    const = lambda i: (0, 0)
    mod_idx = lambda i: (i // tiles_per_seq, 0, 0)
    in_specs = [pl.BlockSpec((tm, d), row),
                pl.BlockSpec((1, mod_rows, d), mod_idx),
                pl.BlockSpec((1, mod_rows, d), mod_idx),
                pl.BlockSpec((1, d), const),
                pl.BlockSpec((d, in_w), const),
                pl.BlockSpec((1, Q_W), const),
                pl.BlockSpec((1, Q_W), const),
                pl.BlockSpec((MXU, MXU), const)]
    f32_out = jax.ShapeDtypeStruct((n, Q_W), F32)
    f32_spec = pl.BlockSpec((tm, Q_W), row)
    if prompt:
        nb = nt // tiles_per_seq
        out_shape = [f32_out, f32_out, jax.ShapeDtypeStruct((n, conv_ch), F32),
                     jax.ShapeDtypeStruct((n, Q_W), BF16), jax.ShapeDtypeStruct((n, Q_W), BF16),
                     jax.ShapeDtypeStruct((nb, tiles_per_seq, ATT_W, tm), BF16)]
        out_specs = [f32_spec, f32_spec, pl.BlockSpec((tm, conv_ch), row),
                     f32_spec, f32_spec,
                     pl.BlockSpec((1, 1, ATT_W, tm), lambda i: (i // tiles_per_seq, i % tiles_per_seq, 0, 0))]
    else:
        out_shape = [f32_out, f32_out, f32_out, jax.ShapeDtypeStruct((n, conv_ch), F32)]
        out_specs = [f32_spec, f32_spec, f32_spec, pl.BlockSpec((tm, conv_ch), row)]
    return pl.pallas_call(
        functools.partial(_proj_body, prompt=prompt),
        grid=(nt,),
        in_specs=in_specs,
        out_specs=out_specs,
        out_shape=out_shape,
        compiler_params=_cparams(("arbitrary",)),
        name="proj_prompt" if prompt else "proj_sample",
    )(x2, sh, sc, gmix, w_in_b, gq_t, gk_t, pm)


def _lambda(lamv, lam_init):
    a = jnp.sum(lamv[0:1] * lamv[1:2], axis=-1, keepdims=True)
    b = jnp.sum(lamv[2:3] * lamv[3:4], axis=-1, keepdims=True)
    return jnp.exp(a) - jnp.exp(b) + lam_init


def _pattn_body(slopes_ref, q_ref, k_ref, vt_ref, bias_ref, lamv_ref, gatt_ref, o_ref,
                qpad_ref, m_ref, l_ref, acc_ref, *, t, lam_init):
    h = pl.program_id(1)
    i = pl.program_id(2)
    slope = slopes_ref[h]
    q = q_ref[0]
    lane = lax.broadcasted_iota(jnp.int32, q.shape, 1)
    zero = jnp.zeros_like(q)
    qpad_ref[0:t, :] = jnp.where(lane < QK_DIM, q, zero)
    qpad_ref[t:2 * t, :] = jnp.where(lane >= QK_DIM, q, zero)
    m_ref[...] = jnp.full(m_ref.shape, NEG, F32)
    l_ref[...] = jnp.zeros(l_ref.shape, F32)
    acc_ref[...] = jnp.zeros(acc_ref.shape, F32)

    def step(j, masked):
        kblk = k_ref[0, pl.ds(pl.multiple_of(j * t, t), t), :]
        s = _dot_nt(kblk, qpad_ref[...]) + bias_ref[0]
        if masked:
            r = lax.broadcasted_iota(jnp.int32, s.shape, 0)
            c = lax.broadcasted_iota(jnp.int32, s.shape, 1)
            s = jnp.where(r <= jnp.where(c >= t, c - t, c), s, NEG)
        cj = slope * (j * t).astype(F32)
        m_old = m_ref[...]
        m_new = jnp.maximum(m_old, jnp.max(s, axis=0, keepdims=True) + cj)
        alpha = jnp.exp(m_old - m_new)
        p = jnp.exp(s - (m_new - cj))
        l_ref[...] = alpha * l_ref[...] + jnp.sum(p, axis=0, keepdims=True)
        acc_ref[...] = alpha * acc_ref[...] + _dot(vt_ref[0, j], p.astype(BF16))
        m_ref[...] = m_new

    def body(j, carry):
        step(j, False)
        return carry

    lax.fori_loop(0, i, body, 0)
    step(i, True)

    o = acc_ref[...] * (1.0 / l_ref[...])
    lam = _lambda(lamv_ref[...], lam_init)
    d = o[:, :t] - lam * o[:, t:]
    ms = jnp.mean(d * d, axis=0, keepdims=True)
    out = d * lax.rsqrt(ms + EPS) * (gatt_ref[...] * (1.0 - lam_init))
    o_ref[0] = out.T.astype(BF16)


def _prompt_attention(qs, kb, vt, bias, slopes, lamv, gatt_col, *, t, lam_init):
    b, s, _ = qs.shape
    nq = s // t
    return pl.pallas_call(
        functools.partial(_pattn_body, t=t, lam_init=lam_init),
        grid_spec=pltpu.PrefetchScalarGridSpec(
            num_scalar_prefetch=0,
            grid=(b, N_HEADS, nq),
            in_specs=[pl.BlockSpec(memory_space=pltpu.SMEM),
                      pl.BlockSpec((1, t, LANE), lambda bb, h, i: (bb, i, h)),
                      pl.BlockSpec((1, s, LANE), lambda bb, h, i: (bb, 0, h)),
                      pl.BlockSpec((1, nq, V_DIM, t), lambda bb, h, i: (bb, 0, h, 0)),
                      pl.BlockSpec((1, t, 2 * t), lambda bb, h, i: (h, 0, 0)),
                      pl.BlockSpec((4, QK_DIM), lambda bb, h, i: (0, 0)),
                      pl.BlockSpec((V_DIM, 1), lambda bb, h, i: (0, 0))],
            out_specs=pl.BlockSpec((1, t, LANE), lambda bb, h, i: (bb, i, h)),
            scratch_shapes=[pltpu.VMEM((2 * t, LANE), BF16),
                            pltpu.VMEM((1, 2 * t), F32),
                            pltpu.VMEM((1, 2 * t), F32),
                            pltpu.VMEM((V_DIM, 2 * t), F32)]),
        out_shape=jax.ShapeDtypeStruct((b, s, ATT_W), BF16),
        compiler_params=_cparams(("arbitrary", "arbitrary", "arbitrary")),
        name="prompt_attention",
    )(slopes, qs, kb, vt, bias, lamv, gatt_col)


def _sattn_body(pt_ref, wq_ref, knew_ref, vnew_ref, slope_ref, qpos_ref, lamv_ref, gatt_ref,
                ck_hbm, cv_hbm, o_ref, kbuf, vbuf, sem, m_ref, l_ref, acc_ref, pad_ref,
                *, cp, nc, layer, t_new, lam_init):
    b = pl.program_id(0)
    c = pl.program_id(1)
    g = b * nc + c
    total = pl.num_programs(0) * nc
    slot = g % 2
    rows = cp * PAGE

    def copies(bb, cc, sl):
        out = []
        for p in range(cp):
            page = pt_ref[bb, cc * cp + p]
            dst = pl.ds(p * PAGE, PAGE)
            out.append(pltpu.make_async_copy(ck_hbm.at[layer, page], kbuf.at[sl, dst], sem.at[0, sl]))
            out.append(pltpu.make_async_copy(cv_hbm.at[layer, page], vbuf.at[sl, dst], sem.at[1, sl]))
        return out

    @pl.when(g == 0)
    def _():
        for cpy in copies(b, c, slot):
            cpy.start()

    @pl.when(g + 1 < total)
    def _():
        gn = g + 1
        for cpy in copies(gn // nc, gn % nc, 1 - slot):
            cpy.start()

    @pl.when(c == 0)
    def _():
        m_ref[...] = jnp.full(m_ref.shape, NEG, F32)
        l_ref[...] = jnp.zeros(l_ref.shape, F32)
        acc_ref[...] = jnp.zeros(acc_ref.shape, F32)

    for cpy in copies(b, c, slot):
        cpy.wait()

    wq = wq_ref[0]
    slope = slope_ref[...]
    qpos = qpos_ref[...]

    def update(s, vals):
        m_old = m_ref[...]
        m_new = jnp.maximum(m_old, jnp.max(s, axis=-1, keepdims=True))
        alpha = jnp.exp(m_old - m_new)
        p = jnp.exp(s - m_new)
        l_ref[...] = alpha * l_ref[...] + jnp.sum(p, axis=-1, keepdims=True)
        acc_ref[...] = alpha * acc_ref[...] + _dot(p.astype(BF16), vals)
        m_ref[...] = m_new

    kpos = (c * rows + lax.broadcasted_iota(jnp.int32, (1, rows), 1)).astype(F32)
    s = _dot_nt(wq, kbuf[slot].astype(BF16)) - slope * (qpos - kpos)
    update(s, vbuf[slot].astype(BF16))

    @pl.when(c == nc - 1)
    def _():
        pad_ref[...] = jnp.zeros(pad_ref.shape, F32)
        pad_ref[0:t_new, :] = knew_ref[0]
        s2 = _dot_nt(wq, pad_ref[...].astype(BF16))
        tk = lax.broadcasted_iota(jnp.int32, s2.shape, 1)
        tq = lax.broadcasted_iota(jnp.int32, s2.shape, 0) % t_new
        s2 = jnp.where(tk <= tq, s2 - slope * (tq - tk).astype(F32), NEG)
        pad_ref[0:t_new, :] = vnew_ref[0]
        update(s2, pad_ref[...].astype(BF16))

        o = acc_ref[...] * (1.0 / l_ref[...])
        lam = _lambda(lamv_ref[...], lam_init)
        gatt = gatt_ref[...] * (1.0 - lam_init)
        for h in range(N_HEADS):
            r0 = 2 * h * t_new
            o1 = o[r0:r0 + t_new, h * V_DIM:(h + 1) * V_DIM]
            o2 = o[r0 + t_new:r0 + 2 * t_new, h * V_DIM:(h + 1) * V_DIM]
            d = o1 - lam * o2
            ms = jnp.mean(d * d, axis=-1, keepdims=True)
            o_ref[0, :, h * V_DIM:(h + 1) * V_DIM] = d * lax.rsqrt(ms + EPS) * gatt


def _sample_attention(page_table, wq, k_new, v_new, slope_col, qpos_col, lamv, gatt_row,
                      cache_k4, cache_v4, *, layer, lam_init):
    db, t_new, _ = k_new.shape
    n_pages = page_table.shape[1]
    cp = math.gcd(n_pages, 16)
    nc = n_pages // cp
    rows_q = wq.shape[1]
    const2 = lambda b, c, pt: (0, 0)
    per_b = lambda b, c, pt: (b, 0, 0)
    return pl.pallas_call(
        functools.partial(_sattn_body, cp=cp, nc=nc, layer=layer, t_new=t_new, lam_init=lam_init),
        grid_spec=pltpu.PrefetchScalarGridSpec(
            num_scalar_prefetch=1,
            grid=(db, nc),
            in_specs=[pl.BlockSpec((1, rows_q, Q_W), per_b),
                      pl.BlockSpec((1, t_new, Q_W), per_b),
                      pl.BlockSpec((1, t_new, ATT_W), per_b),
                      pl.BlockSpec((rows_q, 1), const2),
                      pl.BlockSpec((rows_q, 1), const2),
                      pl.BlockSpec((4, QK_DIM), const2),
                      pl.BlockSpec((1, V_DIM), const2),
                      pl.BlockSpec(memory_space=pl.ANY),
                      pl.BlockSpec(memory_space=pl.ANY)],
            out_specs=pl.BlockSpec((1, t_new, ATT_W), per_b),
            scratch_shapes=[pltpu.VMEM((2, cp * PAGE, Q_W), F32),
                            pltpu.VMEM((2, cp * PAGE, ATT_W), F32),
                            pltpu.SemaphoreType.DMA((2, 2)),
                            pltpu.VMEM((rows_q, 1), F32),
                            pltpu.VMEM((rows_q, 1), F32),
                            pltpu.VMEM((rows_q, ATT_W), F32),
                            pltpu.VMEM((PAGE, Q_W), F32)]),
        out_shape=jax.ShapeDtypeStruct((db, t_new, ATT_W), F32),
        compiler_params=_cparams(("arbitrary", "arbitrary")),
        name="sample_attention",
    )(page_table, wq, k_new, v_new, slope_col, qpos_col, lamv, gatt_row, cache_k4, cache_v4)


def _post_body(x_ref, att_ref, u_ref, halo_ref, gt1_ref, sh2_ref, sc2_ref, wdw_ref, bdw_ref, gln_ref, bln_ref,
               wpw_ref, wout_ref, gffn_ref, wr_ref, br_ref, x1_ref, h2_ref, lg_ref, useq_ref,
               *, tm, halo_from_input):
    i = pl.program_id(1)
    halo = halo_ref[0]
    if not halo_from_input:
        halo = jnp.where(i > 0, halo, jnp.zeros_like(halo))
    useq_ref[0:HALO, :] = halo
    useq_ref[HALO:HALO + tm, :] = u_ref[0]
    wdw = wdw_ref[...]
    acc = jnp.zeros((tm, wdw.shape[1]), F32) + bdw_ref[...]
    off = HALO - (CONV_K - 1)
    for j in range(CONV_K):
        acc = acc + wdw[j:j + 1, :] * useq_ref[off + j:off + j + tm, :]
    mu = jnp.mean(acc, axis=-1, keepdims=True)
    cen = acc - mu
    var = jnp.mean(cen * cen, axis=-1, keepdims=True)
    y = cen * lax.rsqrt(var + EPS) * gln_ref[...] + bln_ref[...]
    y = y * (1.0 / (1.0 + jnp.exp(-y)))
    conv = _dot(y.astype(BF16), wpw_ref[...])
    att = att_ref[0].astype(BF16)
    merged = _dot(att, wout_ref[0:ATT_W, :]) + _dot(conv.astype(BF16), wout_ref[ATT_W:, :])
    x1 = x_ref[0] + gt1_ref[0] * merged
    x1_ref[0] = x1
    h2 = x1 * lax.rsqrt(jnp.mean(x1 * x1, axis=-1, keepdims=True) + EPS) * gffn_ref[...]
    h2 = h2 * (1.0 + sc2_ref[0]) + sh2_ref[0]
    hi, lo = _split_bf16(h2)
    h2_ref[0] = hi
    z = _dot(hi, wr_ref[...]) + _dot(lo, wr_ref[...])
    lg_ref[0] = z[:, :LANE] + z[:, LANE:] + br_ref[...]


def _post(x3, att3, u3, halo_src, gt1, sh2, sc2, wdw, bdw, gln, bln, wpw_b, wout_b, gffn, wr2, br,
          *, tm, halo_from_input):
    nb, s, d = x3.shape
    nt = s // tm
    ch = u3.shape[-1]
    tile = lambda b, i: (b, i, 0)
    mod = lambda b, i: (b, 0, 0)
    const = lambda b, i: (0, 0)
    if halo_from_input:
        halo_idx = lambda b, i: (b, 0, 0)
    else:
        halo_idx = lambda b, i: (b, jnp.maximum(i * (tm // HALO) - 1, 0), 0)
    return pl.pallas_call(
        functools.partial(_post_body, tm=tm, halo_from_input=halo_from_input),
        grid=(nb, nt),
        in_specs=[pl.BlockSpec((1, tm, d), tile),
                  pl.BlockSpec((1, tm, ATT_W), tile),
                  pl.BlockSpec((1, tm, ch), tile),
                  pl.BlockSpec((1, HALO, ch), halo_idx),
                  pl.BlockSpec((1, 1, d), mod),
                  pl.BlockSpec((1, 1, d), mod),
                  pl.BlockSpec((1, 1, d), mod),
                  pl.BlockSpec((CONV_K, ch), const),
                  pl.BlockSpec((1, ch), const),
                  pl.BlockSpec((1, ch), const),
                  pl.BlockSpec((1, ch), const),
                  pl.BlockSpec((ch, ch), const),
                  pl.BlockSpec((d, d), const),
                  pl.BlockSpec((1, d), const),
                  pl.BlockSpec((d, 2 * LANE), const),
                  pl.BlockSpec((1, LANE), const)],
        out_specs=[pl.BlockSpec((1, tm, d), tile),
                   pl.BlockSpec((1, tm, d), tile),
                   pl.BlockSpec((1, tm, LANE), tile)],
        out_shape=[jax.ShapeDtypeStruct((nb, s, d), F32),
                   jax.ShapeDtypeStruct((nb, s, d), BF16),
                   jax.ShapeDtypeStruct((nb, s, LANE), F32)],
        scratch_shapes=[pltpu.VMEM((HALO + tm, ch), F32)],
        compiler_params=_cparams(("arbitrary", "arbitrary")),
        name="post_sample" if halo_from_input else "post_prompt",
    )(x3, att3, u3, halo_src, gt1, sh2, sc2, wdw, bdw, gln, bln, wpw_b, wout_b, gffn, wr2, br)


def _combine_weights(lg):
    lane = lax.broadcasted_iota(jnp.int32, lg.shape, 1).astype(F32)
    big = 1e4
    gl = jnp.where(lane < N_GROUPS, lg, NEG)
    gmax = jnp.max(gl, axis=-1, keepdims=True)
    gsum = jnp.sum(jnp.exp(gl - gmax), axis=-1, keepdims=True)
    g_val = 1.0 / gsum
    g_idx = jnp.min(jnp.where(gl == gmax, lane, big), axis=-1, keepdims=True)
    lo = N_GROUPS + EXPERTS_PER_GROUP * g_idx
    el = jnp.where(lane >= lo, jnp.where(lane < lo + EXPERTS_PER_GROUP, lg, NEG), NEG)
    v1 = jnp.max(el, axis=-1, keepdims=True)
    i1 = jnp.min(jnp.where(el == v1, lane, big), axis=-1, keepdims=True)
    el2 = jnp.where(lane == i1, NEG, el)
    v2 = jnp.max(el2, axis=-1, keepdims=True)
    i2 = jnp.min(jnp.where(el2 == v2, lane, big), axis=-1, keepdims=True)
    e21 = jnp.exp(v2 - v1)
    w1 = g_val / (1.0 + e21)
    w2 = w1 * e21
    return jnp.where(lane == i1, w1, 0.0) + jnp.where(lane == i2, w2, 0.0)


def _moe_body(x1_ref, h2_ref, lg_ref, gt2_ref, wg_ref, wu_ref, wd_ref, o_ref):
    comb = _combine_weights(lg_ref[0])
    h2 = h2_ref[0]
    y = jnp.zeros(x1_ref.shape[1:], F32)
    for e in range(N_EXPERTS):
        gate = _dot(h2, wg_ref[e])
        up = _dot(h2, wu_ref[e])
        hid = gate * (1.0 / (1.0 + jnp.exp(-gate))) * up
        ce = comb[:, N_GROUPS + e:N_GROUPS + e + 1]
        y = y + _dot((hid * ce).astype(BF16), wd_ref[e])
    o_ref[0] = x1_ref[0] + gt2_ref[0] * y


def _moe(x1, h2, lg, gt2, wg_b, wu_b, wd_b, *, tm):
    nb, s, d = x1.shape
    nt = s // tm
    tile = lambda b, i: (b, i, 0)
    const3 = lambda b, i: (0, 0, 0)
    single = pl.Buffered(1)
    return pl.pallas_call(
        _moe_body,
        grid=(nb, nt),
        in_specs=[pl.BlockSpec((1, tm, d), tile),
                  pl.BlockSpec((1, tm, d), tile),
                  pl.BlockSpec((1, tm, LANE), tile),
                  pl.BlockSpec((1, 1, d), lambda b, i: (b, 0, 0)),
                  pl.BlockSpec(wg_b.shape, const3, pipeline_mode=single),
                  pl.BlockSpec(wu_b.shape, const3, pipeline_mode=single),
                  pl.BlockSpec(wd_b.shape, const3, pipeline_mode=single)],
        out_specs=pl.BlockSpec((1, tm, d), tile),
        out_shape=jax.ShapeDtypeStruct((nb, s, d), F32),
        compiler_params=_cparams(("arbitrary", "arbitrary")),
        name="moe",
    )(x1, h2, lg, gt2, wg_b, wu_b, wd_b)


def _alibi_slopes():
    return 2.0 ** (-8.0 * jnp.arange(1, N_HEADS + 1, dtype=F32) / N_HEADS)


def kernel(x_prompt, x_sample, c_prompt, c_sample, cache_k, cache_v, state_conv, page_table, w_ada, b_ada, g_norm_mix, w_in, g_q, g_k, lambda_q1, lambda_k1, lambda_q2, lambda_k2, g_attn_out, w_dw, b_dw, g_conv_ln, b_conv_ln, w_conv_pw, w_out, g_norm_ffn, w_router_group, b_router_group, w_router_expert, b_router_expert, w_exp_gate, w_exp_up, w_exp_down):
    depth = w_in.shape[0]
    bsz, seq, d = x_prompt.shape
    db, t_new, _ = x_sample.shape
    n_pages = page_table.shape[1]
    past = n_pages * PAGE
    n_phys = cache_k.shape[1]
    conv_ch = w_dw.shape[-1]
    t_att = min(512, seq)
    tm_post = min(256, seq)
    tm_moe = min(256, seq)

    slopes = _alibi_slopes()
    blk = jnp.arange(MXU) // QK_DIM
    pm = jnp.where(blk[:, None] == blk[None, :], 1.0 / QK_DIM, 0.0).astype(BF16)
    bias = slopes[:, None, None] * jnp.broadcast_to(
        jnp.arange(t_att, dtype=F32)[None, :, None], (N_HEADS, t_att, 2 * t_att))
    rows_q = 2 * N_HEADS * t_new
    row_head = jnp.arange(rows_q) // (2 * t_new)
    slope_col = slopes[row_head].reshape(rows_q, 1)
    qpos_col = (past + jnp.arange(rows_q) % t_new).astype(F32).reshape(rows_q, 1)
    cache_k4 = cache_k.reshape(depth, n_phys, PAGE, Q_W)
    cache_v4 = cache_v.reshape(depth, n_phys, PAGE, ATT_W)

    yp = x_prompt
    ys = x_sample
    kp_l, vp_l, cp_l, ks_l, vs_l, cs_l = [], [], [], [], [], []
    for l in range(depth):
        lam_init = 0.8 - 0.6 * math.exp(-0.3 * l)
        lamv = jnp.stack([lambda_q1[l], lambda_k1[l], lambda_q2[l], lambda_k2[l]]).astype(F32)
        w_in_b = w_in[l].astype(BF16)
        wpw_b = w_conv_pw[l].astype(BF16)
        wout_b = w_out[l].astype(BF16)
        wg_b = w_exp_gate[l].astype(BF16)
        wu_b = w_exp_up[l].astype(BF16)
        wd_b = w_exp_down[l].astype(BF16)
        gq_t = jnp.tile(g_q[l], 2 * N_HEADS).reshape(1, Q_W)
        gk_t = jnp.tile(g_k[l], 2 * N_HEADS).reshape(1, Q_W)
        gmix = g_norm_mix[l].reshape(1, d)
        gffn = g_norm_ffn[l].reshape(1, d)
        wr = jnp.zeros((d, LANE), F32)
        wr = wr.at[:, :N_GROUPS].set(w_router_group[l]).at[:, N_GROUPS:N_GROUPS + N_EXPERTS].set(w_router_expert[l])
        wr_hi = wr.astype(BF16)
        wr2 = jnp.concatenate([wr_hi, (wr - wr_hi.astype(F32)).astype(BF16)], axis=1)
        br = jnp.zeros((1, LANE), F32)
        br = br.at[0, :N_GROUPS].set(b_router_group[l]).at[0, N_GROUPS:N_GROUPS + N_EXPERTS].set(b_router_expert[l])

        mod = _adaln(jnp.concatenate([c_prompt, c_sample], axis=0), w_ada[l], b_ada[l])
        mod = mod.reshape(bsz + db, 6, 1, d)
        modp = [mod[:bsz, j] for j in range(6)]
        mods = [mod[bsz:, j] for j in range(6)]

        kp, vp, up, qs, kb, vt = _proj(
            yp.reshape(bsz * seq, d), modp[0], modp[1], gmix, w_in_b, gq_t, gk_t, pm,
            tm=t_att, tiles_per_seq=seq // t_att, prompt=True)
        att_p = _prompt_attention(
            qs.reshape(bsz, seq, Q_W), kb.reshape(bsz, seq, Q_W), vt, bias, slopes, lamv,
            g_attn_out[l].reshape(V_DIM, 1), t=t_att, lam_init=lam_init)
        up3 = up.reshape(bsz, seq, conv_ch)
        x1p, h2p, lgp = _post(
            yp, att_p, up3, up3, modp[2], modp[3], modp[4], w_dw[l], b_dw[l].reshape(1, conv_ch),
            g_conv_ln[l].reshape(1, conv_ch), b_conv_ln[l].reshape(1, conv_ch), wpw_b, wout_b, gffn, wr2, br,
            tm=tm_post, halo_from_input=False)
        yp = _moe(x1p, h2p, lgp, modp[5], wg_b, wu_b, wd_b, tm=tm_moe)

        n_s = db * t_new
        rep = lambda m: jnp.broadcast_to(m, (db, t_new, d)).reshape(1, n_s, d)
        qsm, ksm, vsm, usm = _proj(
            ys.reshape(n_s, d), rep(mods[0]), rep(mods[1]), gmix, w_in_b, gq_t, gk_t, pm,
            tm=n_s, tiles_per_seq=1, prompt=False)
        q5 = (qsm * ATT_SCALE).reshape(db, t_new, 2 * N_HEADS, QK_DIM)
        eye = jnp.eye(2 * N_HEADS, dtype=F32)
        wq = (q5.transpose(0, 2, 1, 3)[:, :, :, None, :] * eye[None, :, None, :, None])
        wq = wq.reshape(db, rows_q, Q_W).astype(BF16)
        att_s = _sample_attention(
            page_table, wq, ksm.reshape(db, t_new, Q_W), vsm.reshape(db, t_new, ATT_W), slope_col, qpos_col,
            lamv, g_attn_out[l].reshape(1, V_DIM), cache_k4, cache_v4, layer=l, lam_init=lam_init)
        us3 = usm.reshape(db, t_new, conv_ch)
        state_pad = jnp.pad(state_conv[l], ((0, 0), (HALO - (CONV_K - 1), 0), (0, 0)))
        x1s, h2s, lgs = _post(
            ys, att_s, us3, state_pad, mods[2], mods[3], mods[4], w_dw[l], b_dw[l].reshape(1, conv_ch),
            g_conv_ln[l].reshape(1, conv_ch), b_conv_ln[l].reshape(1, conv_ch), wpw_b, wout_b, gffn, wr2, br,
            tm=t_new, halo_from_input=True)
        ys = _moe(x1s, h2s, lgs, mods[5], wg_b, wu_b, wd_b, tm=t_new)

        kp_l.append(kp.reshape(bsz, seq, N_HEADS, 2, QK_DIM))
        vp_l.append(vp.reshape(bsz, seq, N_HEADS, V_DIM))
        cp_l.append(up3[:, seq - (CONV_K - 1):])
        ks_l.append(ksm.reshape(db, t_new, N_HEADS, 2, QK_DIM))
        vs_l.append(vsm.reshape(db, t_new, N_HEADS, V_DIM))
        cs_l.append(jnp.concatenate([state_conv[l], us3], axis=1)[:, t_new:])

    return (yp, ys, jnp.stack(kp_l), jnp.stack(vp_l), jnp.stack(cp_l),
            jnp.stack(ks_l), jnp.stack(vs_l), jnp.stack(cs_l))
```

```python
import functools
import math

import jax
import jax.numpy as jnp
from jax import lax
from jax.experimental import pallas as pl
from jax.experimental.pallas import tpu as pltpu

F32 = jnp.float32
BF16 = jnp.bfloat16

N_HEADS = 4
QK_DIM = 64
V_DIM = 128
ATT_W = N_HEADS * V_DIM
Q_W = N_HEADS * 2 * QK_DIM
CONV_K = 31
N_GROUPS = 4
EXPERTS_PER_GROUP = 4
N_EXPERTS = N_GROUPS * EXPERTS_PER_GROUP
EPS = 1e-6
ATT_SCALE = 1.0 / math.sqrt(QK_DIM)
PAGE = 128
NEG = -1e30
LOG2E = math.log2(math.e)
ALIBI_SLOPES = tuple(2.0 ** (-8.0 * (i + 1) / N_HEADS) for i in range(N_HEADS))

LANE = 128
MXU = 256
HALO = 32
VT_ROWS = V_DIM + 16
VMEM_LIMIT = 56 * 1024 * 1024


def _cparams(sem):
    return pltpu.CompilerParams(dimension_semantics=sem, vmem_limit_bytes=VMEM_LIMIT)


def _dot(a, b):
    return jnp.dot(a, b, preferred_element_type=F32)


def _dot_nt(a, b):
    return lax.dot_general(a, b, (((1,), (1,)), ((), ())), preferred_element_type=F32)


def _split_bf16(x):
    hi = x.astype(BF16)
    lo = (x - hi.astype(F32)).astype(BF16)
    return hi, lo


def _sigmoid(x):
    return 1.0 / (1.0 + jnp.exp(-x))


def _adaln_body(c_ref, w_ref, b_ref, o_ref):
    c = c_ref[...]
    o_ref[...] = jnp.dot(c * _sigmoid(c), w_ref[...], preferred_element_type=F32,
                         precision=lax.Precision.HIGHEST) + b_ref[...]


def _adaln(c, w_ada, b_ada):
    nb, d = c.shape
    n_out = w_ada.shape[1]
    bn = 1024
    return pl.pallas_call(
        _adaln_body,
        grid=(n_out // bn,),
        in_specs=[pl.BlockSpec((nb, d), lambda j: (0, 0)),
                  pl.BlockSpec((d, bn), lambda j: (0, j)),
                  pl.BlockSpec((1, bn), lambda j: (0, j))],
        out_specs=pl.BlockSpec((nb, bn), lambda j: (0, j)),
        out_shape=jax.ShapeDtypeStruct((nb, n_out), F32),
        compiler_params=_cparams(("arbitrary",)),
        name="adaln",
    )(c, w_ada, b_ada.reshape(1, n_out))


def _proj_body(x_ref, sh_ref, sc_ref, gmix_ref, w_ref, gq_ref, gk_ref, pm_ref, *out_refs, prompt, tiles_per_seq):
    x = x_ref[...]
    tm = x.shape[0]
    h = x * lax.rsqrt(jnp.mean(x * x, axis=-1, keepdims=True) + EPS) * gmix_ref[...]
    h = h * (1.0 + sc_ref[0]) + sh_ref[0]
    hb = h.astype(BF16)
    pm = pm_ref[...]

    def chunk_rms(z, g):
        outs = []
        for c in range(Q_W // MXU):
            zc = z[:, c * MXU:(c + 1) * MXU]
            hi, lo = _split_bf16(zc * zc)
            ms = _dot(hi, pm) + _dot(lo, pm)
            outs.append(zc * lax.rsqrt(ms + EPS))
        return jnp.concatenate(outs, axis=-1) * g

    qn = chunk_rms(_dot(hb, w_ref[:, 0:Q_W]), gq_ref[...])
    kn = chunk_rms(_dot(hb, w_ref[:, Q_W:2 * Q_W]), gk_ref[...])
    v = _dot(hb, w_ref[:, 2 * Q_W:2 * Q_W + ATT_W])
    a = _dot(hb, w_ref[:, 2 * Q_W + ATT_W:])
    cw = a.shape[-1] // 2
    u = a[:, :cw] * _sigmoid(a[:, cw:])
    if not prompt:
        q_ref, k_ref, v_ref, u_ref = out_refs
        q_ref[...] = qn
        k_ref[...] = kn
        v_ref[...] = v
        u_ref[...] = u
        return

    kt_ref, v4_ref, u_ref, qa_ref, ka_ref, vt_ref = out_refs
    u_ref[...] = u
    kt_ref[0] = kn.T
    for hh in range(N_HEADS):
        v4_ref[pl.ds(hh, tm, stride=N_HEADS), :] = v[:, hh * V_DIM:(hh + 1) * V_DIM]
    vt = v.T
    for hh in range(N_HEADS):
        vt_ref[0, 0, hh, 0:V_DIM, :] = vt[hh * V_DIM:(hh + 1) * V_DIM, :].astype(BF16)
        vt_ref[0, 0, hh, V_DIM:VT_ROWS, :] = jnp.ones((VT_ROWS - V_DIM, tm), BF16)
    lane = lax.broadcasted_iota(jnp.int32, (tm, LANE), 1)
    pos = ((pl.program_id(0) % tiles_per_seq) * tm + lax.broadcasted_iota(jnp.int32, (tm, 1), 0)).astype(F32)
    q_aux = jnp.where(lane < QK_DIM + 3, 1.0, 0.0)
    qs = qn * (ATT_SCALE * LOG2E)
    for hh in range(N_HEADS):
        b = pos * (ALIBI_SLOPES[hh] * LOG2E)
        b_hi = b.astype(BF16).astype(F32)
        b_mid = (b - b_hi).astype(BF16).astype(F32)
        b_lo = b - b_hi - b_mid
        k_aux = jnp.where(lane == QK_DIM, b_hi,
                          jnp.where(lane == QK_DIM + 1, b_mid, jnp.where(lane == QK_DIM + 2, b_lo, 0.0)))
        qg = qs[:, hh * LANE:(hh + 1) * LANE]
        kg = kn[:, hh * LANE:(hh + 1) * LANE]
        for mp in range(2):
            if mp == 1:
                qg = pltpu.roll(qg, QK_DIM, 1)
                kg = pltpu.roll(kg, QK_DIM, 1)
            c0 = (2 * hh + mp) * LANE
            qa_ref[:, c0:c0 + LANE] = jnp.where(lane < QK_DIM, qg, q_aux).astype(BF16)
            ka_ref[:, c0:c0 + LANE] = jnp.where(lane < QK_DIM, kg, k_aux).astype(BF16)


def _proj(x2, sh, sc, gmix, w_in_b, gq_t, gk_t, pm, *, tm, tiles_per_seq, prompt):
    n, d = x2.shape
    nt = n // tm
    in_w = w_in_b.shape[1]
    conv_ch = (in_w - 2 * Q_W - ATT_W) // 2
    mod_rows = sh.shape[1]
    row = lambda i: (i, 0)
    const = lambda i: (0, 0)
    mod_idx = lambda i: (i // tiles_per_seq, 0, 0)
    in_specs = [pl.BlockSpec((tm, d), row),
                pl.BlockSpec((1, mod_rows, d), mod_idx),
                pl.BlockSpec((1, mod_rows, d), mod_idx),
                pl.BlockSpec((1, d), const),
                pl.BlockSpec((d, in_w), const),
                pl.BlockSpec((1, Q_W), const),
                pl.BlockSpec((1, Q_W), const),
                pl.BlockSpec((MXU, MXU), const)]
    if prompt:
        nb = nt // tiles_per_seq
        out_shape = [jax.ShapeDtypeStruct((nb, Q_W, tiles_per_seq * tm), F32),
                     jax.ShapeDtypeStruct((n * N_HEADS, V_DIM), F32),
                     jax.ShapeDtypeStruct((n, conv_ch), F32),
                     jax.ShapeDtypeStruct((n, 2 * Q_W), BF16),
                     jax.ShapeDtypeStruct((n, 2 * Q_W), BF16),
                     jax.ShapeDtypeStruct((nb, tiles_per_seq, N_HEADS, VT_ROWS, tm), BF16)]
        out_specs = [pl.BlockSpec((1, Q_W, tm), lambda i: (i // tiles_per_seq, 0, i % tiles_per_seq)),
                     pl.BlockSpec((tm * N_HEADS, V_DIM), row),
                     pl.BlockSpec((tm, conv_ch), row),
                     pl.BlockSpec((tm, 2 * Q_W), row),
                     pl.BlockSpec((tm, 2 * Q_W), row),
                     pl.BlockSpec((1, 1, N_HEADS, VT_ROWS, tm),
                                  lambda i: (i // tiles_per_seq, i % tiles_per_seq, 0, 0, 0))]
    else:
        f32_out = jax.ShapeDtypeStruct((n, Q_W), F32)
        f32_spec = pl.BlockSpec((tm, Q_W), row)
        out_shape = [f32_out, f32_out, f32_out, jax.ShapeDtypeStruct((n, conv_ch), F32)]
        out_specs = [f32_spec, f32_spec, f32_spec, pl.BlockSpec((tm, conv_ch), row)]
    return pl.pallas_call(
        functools.partial(_proj_body, prompt=prompt, tiles_per_seq=tiles_per_seq),
        grid=(nt,),
        in_specs=in_specs,
        out_specs=out_specs,
        out_shape=out_shape,
        compiler_params=_cparams(("arbitrary",)),
        name="proj_prompt" if prompt else "proj_sample",
    )(x2, sh, sc, gmix, w_in_b, gq_t, gk_t, pm)


def _lambda(lamv, lam_init):
    a = jnp.sum(lamv[0:1] * lamv[1:2], axis=-1, keepdims=True)
    b = jnp.sum(lamv[2:3] * lamv[3:4], axis=-1, keepdims=True)
    return jnp.exp(a) - jnp.exp(b) + lam_init


def _pattn_body(q_ref, k_ref, vt_ref, lamv_ref, gatt_ref, o_ref, m_ref, acc_ref, s0_ref, s1_ref, mb0_ref, mb1_ref,
                *, t, lam_init):
    i = pl.program_id(2)
    m_ref[...] = jnp.full(m_ref.shape, NEG, F32)
    acc_ref[...] = jnp.zeros(acc_ref.shape, F32)

    def scores(j, s_ref, mb_ref):
        kblk = k_ref[0, pl.ds(pl.multiple_of(j * t, t), t), :]
        for mp in range(2):
            sl = slice(mp * t, (mp + 1) * t)
            s = _dot_nt(kblk[:, mp * LANE:(mp + 1) * LANE], q_ref[0, :, mp * LANE:(mp + 1) * LANE])
            s_ref[:, sl] = s
            mb_ref[:, sl] = jnp.max(s, axis=0, keepdims=True)

    def accumulate(j, s_ref, mb_ref, masked):
        vt = vt_ref[0, j, 0]
        for mp in range(2):
            sl = slice(mp * t, (mp + 1) * t)
            s = s_ref[:, sl]
            if masked:
                r = lax.broadcasted_iota(jnp.int32, s.shape, 0)
                c = lax.broadcasted_iota(jnp.int32, s.shape, 1)
                s = jnp.where(r <= c, s, NEG)
                mb = jnp.max(s, axis=0, keepdims=True)
            else:
                mb = mb_ref[:, sl]
            m_old = m_ref[:, sl]
            m_new = jnp.maximum(m_old, mb)
            alpha = jnp.exp2(m_old - m_new)
            p = jnp.exp2(s - m_new).astype(BF16)
            acc_ref[:, sl] = alpha * acc_ref[:, sl] + _dot(vt, p)
            m_ref[:, sl] = m_new

    scores(0, s0_ref, mb0_ref)

    def body(jj, carry):
        j = 2 * jj
        scores(j + 1, s1_ref, mb1_ref)
        accumulate(j, s0_ref, mb0_ref, False)
        scores(j + 2, s0_ref, mb0_ref)
        accumulate(j + 1, s1_ref, mb1_ref, False)
        return carry

    lax.fori_loop(0, i // 2, body, 0)

    @pl.when(i % 2 == 0)
    def _():
        accumulate(i, s0_ref, mb0_ref, True)

    @pl.when(i % 2 == 1)
    def _():
        scores(i, s1_ref, mb1_ref)
        accumulate(i - 1, s0_ref, mb0_ref, False)
        accumulate(i, s1_ref, mb1_ref, True)

    acc = acc_ref[...]
    o = acc[0:V_DIM] * (1.0 / acc[V_DIM:V_DIM + 1])
    lam = _lambda(lamv_ref[...], lam_init)
    d = o[:, :t] - lam * o[:, t:]
    ms = jnp.mean(d * d, axis=0, keepdims=True)
    out = d * lax.rsqrt(ms + EPS) * (gatt_ref[...] * (1.0 - lam_init))
    o_ref[0] = out.T.astype(BF16)


def _prompt_attention(qa, ka, vt, lamv, gatt_col, *, t, lam_init):
    b, s, _ = qa.shape
    nq = s // t
    return pl.pallas_call(
        functools.partial(_pattn_body, t=t, lam_init=lam_init),
        grid=(b, N_HEADS, nq),
        in_specs=[pl.BlockSpec((1, t, 2 * LANE), lambda bb, h, i: (bb, i, h)),
                  pl.BlockSpec((1, s, 2 * LANE), lambda bb, h, i: (bb, 0, h)),
                  pl.BlockSpec((1, nq, 1, VT_ROWS, t), lambda bb, h, i: (bb, 0, h, 0, 0)),
                  pl.BlockSpec((4, QK_DIM), lambda bb, h, i: (0, 0)),
                  pl.BlockSpec((V_DIM, 1), lambda bb, h, i: (0, 0))],
        out_specs=pl.BlockSpec((1, t, LANE), lambda bb, h, i: (bb, i, h)),
        scratch_shapes=[pltpu.VMEM((1, 2 * t), F32),
                        pltpu.VMEM((VT_ROWS, 2 * t), F32),
                        pltpu.VMEM((t, 2 * t), F32),
                        pltpu.VMEM((t, 2 * t), F32),
                        pltpu.VMEM((1, 2 * t), F32),
                        pltpu.VMEM((1, 2 * t), F32)],
        out_shape=jax.ShapeDtypeStruct((b, s, ATT_W), BF16),
        compiler_params=_cparams(("arbitrary", "arbitrary", "arbitrary")),
        name="prompt_attention",
    )(qa, ka, vt, lamv, gatt_col)


def _sattn_body(pt_ref, wq_ref, knew_ref, vnew_ref, slope_ref, qpos_ref, lamv_ref, gatt_ref,
                ck_hbm, cv_hbm, o_ref, kbuf, vbuf, sem, m_ref, l_ref, acc_ref, pad_ref,
                *, cp, nc, layer, t_new, lam_init):
    b = pl.program_id(0)
    c = pl.program_id(1)
    g = b * nc + c
    total = pl.num_programs(0) * nc
    slot = g % 2
    keys = cp * PAGE
    hrows = 2 * t_new

    def copies(bb, cc, sl):
        out = []
        for p in range(cp):
            page = pt_ref[bb, cc * cp + p]
            out.append(pltpu.make_async_copy(ck_hbm.at[layer, page], kbuf.at[sl, :, pl.ds(p * PAGE, PAGE)],
                                             sem.at[0, sl]))
            out.append(pltpu.make_async_copy(cv_hbm.at[layer, page],
                                             vbuf.at[sl, pl.ds(p * PAGE * N_HEADS, PAGE * N_HEADS), :],
                                             sem.at[1, sl]))
        return out

    @pl.when(g == 0)
    def _():
        for cpy in copies(b, c, slot):
            cpy.start()

    @pl.when(g + 1 < total)
    def _():
        gn = g + 1
        for cpy in copies(gn // nc, gn % nc, 1 - slot):
            cpy.start()

    @pl.when(c == 0)
    def _():
        m_ref[...] = jnp.full(m_ref.shape, NEG, F32)
        l_ref[...] = jnp.zeros(l_ref.shape, F32)
        acc_ref[...] = jnp.zeros(acc_ref.shape, F32)

    for cpy in copies(b, c, slot):
        cpy.wait()

    wq = wq_ref[0]
    slope = slope_ref[...]
    qpos = qpos_ref[...]

    def update(s, head_values):
        m_old = m_ref[...]
        m_new = jnp.maximum(m_old, jnp.max(s, axis=-1, keepdims=True))
        alpha = jnp.exp(m_old - m_new)
        p = jnp.exp(s - m_new)
        l_ref[...] = alpha * l_ref[...] + jnp.sum(p, axis=-1, keepdims=True)
        for hh in range(N_HEADS):
            rs = slice(hh * hrows, (hh + 1) * hrows)
            acc_ref[rs, :] = alpha[rs] * acc_ref[rs, :] + _dot(p[rs], head_values(hh))
        m_ref[...] = m_new

    kpos = (c * keys + lax.broadcasted_iota(jnp.int32, (1, keys), 1)).astype(F32)
    s = _dot(wq, kbuf[slot]) - slope * (qpos - kpos)
    vslot = vbuf.at[slot]
    update(s, lambda hh: vslot[pl.ds(hh, keys, stride=N_HEADS), :])

    @pl.when(c == nc - 1)
    def _():
        pad_ref[...] = jnp.zeros(pad_ref.shape, F32)
        pad_ref[0:t_new, :] = knew_ref[0]
        s2 = _dot_nt(wq, pad_ref[...])
        tk = lax.broadcasted_iota(jnp.int32, s2.shape, 1)
        tq = lax.broadcasted_iota(jnp.int32, s2.shape, 0) % t_new
        s2 = jnp.where(tk <= tq, s2 - slope * (tq - tk).astype(F32), NEG)
        pad_ref[0:t_new, :] = vnew_ref[0]
        update(s2, lambda hh: pad_ref[:, hh * V_DIM:(hh + 1) * V_DIM])

        o = acc_ref[...] * (1.0 / l_ref[...])
        lam = _lambda(lamv_ref[...], lam_init)
        gatt = gatt_ref[...] * (1.0 - lam_init)
        for hh in range(N_HEADS):
            r0 = hh * hrows
            d = o[r0:r0 + t_new] - lam * o[r0 + t_new:r0 + hrows]
            ms = jnp.mean(d * d, axis=-1, keepdims=True)
            o_ref[0, :, hh * V_DIM:(hh + 1) * V_DIM] = d * lax.rsqrt(ms + EPS) * gatt


def _sample_attention(page_table, wq, k_new, v_new, slope_col, qpos_col, lamv, gatt_row,
                      cache_kt, cache_v4, *, layer, lam_init):
    db, t_new, _ = k_new.shape
    n_pages = page_table.shape[1]
    cp = math.gcd(n_pages, 16)
    nc = n_pages // cp
    rows_q = wq.shape[1]
    const2 = lambda b, c, pt: (0, 0)
    per_b = lambda b, c, pt: (b, 0, 0)
    return pl.pallas_call(
        functools.partial(_sattn_body, cp=cp, nc=nc, layer=layer, t_new=t_new, lam_init=lam_init),
        grid_spec=pltpu.PrefetchScalarGridSpec(
            num_scalar_prefetch=1,
            grid=(db, nc),
            in_specs=[pl.BlockSpec((1, rows_q, Q_W), per_b),
                      pl.BlockSpec((1, t_new, Q_W), per_b),
                      pl.BlockSpec((1, t_new, ATT_W), per_b),
                      pl.BlockSpec((rows_q, 1), const2),
                      pl.BlockSpec((rows_q, 1), const2),
                      pl.BlockSpec((4, QK_DIM), const2),
                      pl.BlockSpec((1, V_DIM), const2),
                      pl.BlockSpec(memory_space=pl.ANY),
                      pl.BlockSpec(memory_space=pl.ANY)],
            out_specs=pl.BlockSpec((1, t_new, ATT_W), per_b),
            scratch_shapes=[pltpu.VMEM((2, Q_W, cp * PAGE), F32),
                            pltpu.VMEM((2, cp * PAGE * N_HEADS, V_DIM), F32),
                            pltpu.SemaphoreType.DMA((2, 2)),
                            pltpu.VMEM((rows_q, 1), F32),
                            pltpu.VMEM((rows_q, 1), F32),
                            pltpu.VMEM((rows_q, V_DIM), F32),
                            pltpu.VMEM((PAGE, Q_W), F32)]),
        out_shape=jax.ShapeDtypeStruct((db, t_new, ATT_W), F32),
        compiler_params=_cparams(("arbitrary", "arbitrary")),
        name="sample_attention",
    )(page_table, wq, k_new, v_new, slope_col, qpos_col, lamv, gatt_row, cache_kt, cache_v4)


def _post_body(x_ref, att_ref, u_ref, halo_ref, gt1_ref, sh2_ref, sc2_ref, wdw_ref, bdw_ref, gln_ref, bln_ref,
               wpw_ref, wout_ref, gffn_ref, wr_ref, br_ref, x1_ref, h2_ref, lg_ref, useq_ref,
               *, tm, halo_from_input):
    i = pl.program_id(1)
    halo = halo_ref[0]
    if not halo_from_input:
        halo = jnp.where(i > 0, halo, jnp.zeros_like(halo))
    useq_ref[0:HALO, :] = halo
    useq_ref[HALO:HALO + tm, :] = u_ref[0]
    wdw = wdw_ref[...]
    acc = jnp.zeros((tm, wdw.shape[1]), F32) + bdw_ref[...]
    off = HALO - (CONV_K - 1)
    for j in range(CONV_K):
        acc = acc + wdw[j:j + 1, :] * useq_ref[off + j:off + j + tm, :]
    mu = jnp.mean(acc, axis=-1, keepdims=True)
    cen = acc - mu
    var = jnp.mean(cen * cen, axis=-1, keepdims=True)
    y = cen * lax.rsqrt(var + EPS) * gln_ref[...] + bln_ref[...]
    y = y * _sigmoid(y)
    conv = _dot(y.astype(BF16), wpw_ref[...])
    att = att_ref[0].astype(BF16)
    merged = _dot(att, wout_ref[0:ATT_W, :]) + _dot(conv.astype(BF16), wout_ref[ATT_W:, :])
    x1 = x_ref[0] + gt1_ref[0] * merged
    x1_ref[0] = x1
    h2 = x1 * lax.rsqrt(jnp.mean(x1 * x1, axis=-1, keepdims=True) + EPS) * gffn_ref[...]
    h2 = h2 * (1.0 + sc2_ref[0]) + sh2_ref[0]
    hi, lo = _split_bf16(h2)
    h2_ref[0] = hi
    z = _dot(hi, wr_ref[...]) + _dot(lo, wr_ref[...])
    lg_ref[0] = z[:, :LANE] + z[:, LANE:] + br_ref[...]


def _post(x3, att3, u3, halo_src, gt1, sh2, sc2, wdw, bdw, gln, bln, wpw_b, wout_b, gffn, wr2, br,
          *, tm, halo_from_input):
    nb, s, d = x3.shape
    nt = s // tm
    ch = u3.shape[-1]
    tile = lambda b, i: (b, i, 0)
    mod = lambda b, i: (b, 0, 0)
    const = lambda b, i: (0, 0)
    if halo_from_input:
        halo_idx = lambda b, i: (b, 0, 0)
    else:
        halo_idx = lambda b, i: (b, jnp.maximum(i * (tm // HALO) - 1, 0), 0)
    return pl.pallas_call(
        functools.partial(_post_body, tm=tm, halo_from_input=halo_from_input),
        grid=(nb, nt),
        in_specs=[pl.BlockSpec((1, tm, d), tile),
                  pl.BlockSpec((1, tm, ATT_W), tile),
                  pl.BlockSpec((1, tm, ch), tile),
                  pl.BlockSpec((1, HALO, ch), halo_idx),
                  pl.BlockSpec((1, 1, d), mod),
                  pl.BlockSpec((1, 1, d), mod),
                  pl.BlockSpec((1, 1, d), mod),
                  pl.BlockSpec((CONV_K, ch), const),
                  pl.BlockSpec((1, ch), const),
                  pl.BlockSpec((1, ch), const),
                  pl.BlockSpec((1, ch), const),
                  pl.BlockSpec((ch, ch), const),
                  pl.BlockSpec((d, d), const),
                  pl.BlockSpec((1, d), const),
                  pl.BlockSpec((d, 2 * LANE), const),
                  pl.BlockSpec((1, LANE), const)],
        out_specs=[pl.BlockSpec((1, tm, d), tile),
                   pl.BlockSpec((1, tm, d), tile),
                   pl.BlockSpec((1, tm, LANE), tile)],
        out_shape=[jax.ShapeDtypeStruct((nb, s, d), F32),
                   jax.ShapeDtypeStruct((nb, s, d), BF16),
                   jax.ShapeDtypeStruct((nb, s, LANE), F32)],
        scratch_shapes=[pltpu.VMEM((HALO + tm, ch), F32)],
        compiler_params=_cparams(("arbitrary", "arbitrary")),
        name="post_sample" if halo_from_input else "post_prompt",
    )(x3, att3, u3, halo_src, gt1, sh2, sc2, wdw, bdw, gln, bln, wpw_b, wout_b, gffn, wr2, br)


def _combine_weights(lg):
    lane = lax.broadcasted_iota(jnp.int32, lg.shape, 1).astype(F32)
    big = 1e4
    gl = jnp.where(lane < N_GROUPS, lg, NEG)
    gmax = jnp.max(gl, axis=-1, keepdims=True)
    gsum = jnp.sum(jnp.exp(gl - gmax), axis=-1, keepdims=True)
    g_val = 1.0 / gsum
    g_idx = jnp.min(jnp.where(gl == gmax, lane, big), axis=-1, keepdims=True)
    lo = N_GROUPS + EXPERTS_PER_GROUP * g_idx
    el = jnp.where(lane >= lo, jnp.where(lane < lo + EXPERTS_PER_GROUP, lg, NEG), NEG)
    v1 = jnp.max(el, axis=-1, keepdims=True)
    i1 = jnp.min(jnp.where(el == v1, lane, big), axis=-1, keepdims=True)
    el2 = jnp.where(lane == i1, NEG, el)
    v2 = jnp.max(el2, axis=-1, keepdims=True)
    i2 = jnp.min(jnp.where(el2 == v2, lane, big), axis=-1, keepdims=True)
    e21 = jnp.exp(v2 - v1)
    w1 = g_val / (1.0 + e21)
    w2 = w1 * e21
    return jnp.where(lane == i1, w1, 0.0) + jnp.where(lane == i2, w2, 0.0)


def _moe_body(x1_ref, h2_ref, lg_ref, gt2_ref, wg_ref, wu_ref, wd_ref, o_ref):
    comb = _combine_weights(lg_ref[0])
    h2 = h2_ref[0]
    y = jnp.zeros(x1_ref.shape[1:], F32)
    for e in range(N_EXPERTS):
        gate = _dot(h2, wg_ref[e])
        up = _dot(h2, wu_ref[e])
        hid = gate * _sigmoid(gate) * up
        ce = comb[:, N_GROUPS + e:N_GROUPS + e + 1]
        y = y + _dot((hid * ce).astype(BF16), wd_ref[e])
    o_ref[0] = x1_ref[0] + gt2_ref[0] * y


def _moe(x1, h2, lg, gt2, wg_b, wu_b, wd_b, *, tm):
    nb, s, d = x1.shape
    nt = s // tm
    mod_rows = gt2.shape[1]
    tile = lambda b, i: (b, i, 0)
    const3 = lambda b, i: (0, 0, 0)
    single = pl.Buffered(1)
    return pl.pallas_call(
        _moe_body,
        grid=(nb, nt),
        in_specs=[pl.BlockSpec((1, tm, d), tile),
                  pl.BlockSpec((1, tm, d), tile),
                  pl.BlockSpec((1, tm, LANE), tile),
                  pl.BlockSpec((1, mod_rows, d), lambda b, i: (b, 0, 0)),
                  pl.BlockSpec(wg_b.shape, const3, pipeline_mode=single),
                  pl.BlockSpec(wu_b.shape, const3, pipeline_mode=single),
                  pl.BlockSpec(wd_b.shape, const3, pipeline_mode=single)],
        out_specs=pl.BlockSpec((1, tm, d), tile),
        out_shape=jax.ShapeDtypeStruct((nb, s, d), F32),
        compiler_params=_cparams(("arbitrary", "arbitrary")),
        name="moe",
    )(x1, h2, lg, gt2, wg_b, wu_b, wd_b)


def kernel(x_prompt, x_sample, c_prompt, c_sample, cache_k, cache_v, state_conv, page_table, w_ada, b_ada, g_norm_mix, w_in, g_q, g_k, lambda_q1, lambda_k1, lambda_q2, lambda_k2, g_attn_out, w_dw, b_dw, g_conv_ln, b_conv_ln, w_conv_pw, w_out, g_norm_ffn, w_router_group, b_router_group, w_router_expert, b_router_expert, w_exp_gate, w_exp_up, w_exp_down):
    depth = w_in.shape[0]
    bsz, seq, d = x_prompt.shape
    db, t_new, _ = x_sample.shape
    n_pages = page_table.shape[1]
    past = n_pages * PAGE
    n_phys = cache_k.shape[1]
    conv_ch = w_dw.shape[-1]
    t_att = min(512, seq)
    tm_post = min(256, seq)
    tm_moe = min(256, seq)
    n_s = db * t_new

    blk = jnp.arange(MXU) // QK_DIM
    pm = jnp.where(blk[:, None] == blk[None, :], 1.0 / QK_DIM, 0.0).astype(BF16)
    rows_q = 2 * N_HEADS * t_new
    row_head = jnp.arange(rows_q) // (2 * t_new)
    slope_col = jnp.asarray(ALIBI_SLOPES, F32)[row_head].reshape(rows_q, 1)
    qpos_col = (past + jnp.arange(rows_q) % t_new).astype(F32).reshape(rows_q, 1)
    cache_kt = jnp.transpose(cache_k, (0, 1, 3, 4, 5, 2)).reshape(depth, n_phys, Q_W, PAGE)
    cache_v4 = cache_v.reshape(depth, n_phys, PAGE * N_HEADS, V_DIM)

    yp = x_prompt
    ys = x_sample
    kp_l, vp_l, cp_l, ks_l, vs_l, cs_l = [], [], [], [], [], []
    for l in range(depth):
        lam_init = 0.8 - 0.6 * math.exp(-0.3 * l)
        lamv = jnp.stack([lambda_q1[l], lambda_k1[l], lambda_q2[l], lambda_k2[l]]).astype(F32)
        w_in_b = w_in[l].astype(BF16)
        wpw_b = w_conv_pw[l].astype(BF16)
        wout_b = w_out[l].astype(BF16)
        wg_b = w_exp_gate[l].astype(BF16)
        wu_b = w_exp_up[l].astype(BF16)
        wd_b = w_exp_down[l].astype(BF16)
        gq_t = jnp.tile(g_q[l], 2 * N_HEADS).reshape(1, Q_W)
        gk_t = jnp.tile(g_k[l], 2 * N_HEADS).reshape(1, Q_W)
        gmix = g_norm_mix[l].reshape(1, d)
        gffn = g_norm_ffn[l].reshape(1, d)
        wr = jnp.zeros((d, LANE), F32)
        wr = wr.at[:, :N_GROUPS].set(w_router_group[l]).at[:, N_GROUPS:N_GROUPS + N_EXPERTS].set(w_router_expert[l])
        wr_hi = wr.astype(BF16)
        wr2 = jnp.concatenate([wr_hi, (wr - wr_hi.astype(F32)).astype(BF16)], axis=1)
        br = jnp.zeros((1, LANE), F32)
        br = br.at[0, :N_GROUPS].set(b_router_group[l]).at[0, N_GROUPS:N_GROUPS + N_EXPERTS].set(b_router_expert[l])
        conv_w = (w_dw[l], b_dw[l].reshape(1, conv_ch), g_conv_ln[l].reshape(1, conv_ch),
                  b_conv_ln[l].reshape(1, conv_ch), wpw_b, wout_b, gffn, wr2, br)

        mod = _adaln(jnp.concatenate([c_prompt, c_sample], axis=0), w_ada[l], b_ada[l])
        mod = mod.reshape(bsz + db, 6, 1, d)
        modp = [mod[:bsz, j] for j in range(6)]
        mods = [mod[bsz:, j] for j in range(6)]

        kt, v4, up, qa, ka, vt = _proj(
            yp.reshape(bsz * seq, d), modp[0], modp[1], gmix, w_in_b, gq_t, gk_t, pm,
            tm=t_att, tiles_per_seq=seq // t_att, prompt=True)
        att_p = _prompt_attention(
            qa.reshape(bsz, seq, 2 * Q_W), ka.reshape(bsz, seq, 2 * Q_W), vt, lamv,
            g_attn_out[l].reshape(V_DIM, 1), t=t_att, lam_init=lam_init)
        up3 = up.reshape(bsz, seq, conv_ch)
        x1p, h2p, lgp = _post(yp, att_p, up3, up3, modp[2], modp[3], modp[4], *conv_w,
                              tm=tm_post, halo_from_input=False)
        yp = _moe(x1p, h2p, lgp, modp[5], wg_b, wu_b, wd_b, tm=tm_moe)

        rep = lambda m: jnp.broadcast_to(m, (db, t_new, d)).reshape(1, n_s, d)
        qsm, ksm, vsm, usm = _proj(
            ys.reshape(n_s, d), rep(mods[0]), rep(mods[1]), gmix, w_in_b, gq_t, gk_t, pm,
            tm=n_s, tiles_per_seq=1, prompt=False)
        q5 = (qsm * ATT_SCALE).reshape(db, t_new, 2 * N_HEADS, QK_DIM)
        eye = jnp.eye(2 * N_HEADS, dtype=F32)
        wq = (q5.transpose(0, 2, 1, 3)[:, :, :, None, :] * eye[None, :, None, :, None])
        wq = wq.reshape(db, rows_q, Q_W)
        att_s = _sample_attention(
            page_table, wq, ksm.reshape(db, t_new, Q_W), vsm.reshape(db, t_new, ATT_W), slope_col, qpos_col,
            lamv, g_attn_out[l].reshape(1, V_DIM), cache_kt, cache_v4, layer=l, lam_init=lam_init)
        us3 = usm.reshape(db, t_new, conv_ch)
        state_pad = jnp.pad(state_conv[l], ((0, 0), (HALO - (CONV_K - 1), 0), (0, 0)))
        x1s, h2s, lgs = _post(ys, att_s, us3, state_pad, mods[2], mods[3], mods[4], *conv_w,
                              tm=t_new, halo_from_input=True)
        ys = _moe(x1s.reshape(1, n_s, d), h2s.reshape(1, n_s, d), lgs.reshape(1, n_s, LANE), rep(mods[5]),
                  wg_b, wu_b, wd_b, tm=n_s).reshape(db, t_new, d)

        kp_l.append(jnp.transpose(kt.reshape(bsz, N_HEADS, 2, QK_DIM, seq), (0, 4, 1, 2, 3)))
        vp_l.append(v4.reshape(bsz, seq, N_HEADS, V_DIM))
        cp_l.append(up3[:, seq - (CONV_K - 1):])
        ks_l.append(ksm.reshape(db, t_new, N_HEADS, 2, QK_DIM))
        vs_l.append(vsm.reshape(db, t_new, N_HEADS, V_DIM))
        cs_l.append(jnp.concatenate([state_conv[l], us3], axis=1)[:, t_new:])

    return (yp, ys, jnp.stack(kp_l), jnp.stack(vp_l), jnp.stack(cp_l),
            jnp.stack(ks_l), jnp.stack(vs_l), jnp.stack(cs_l))
```

```python
import functools
import math

import jax
import jax.numpy as jnp
from jax import lax
from jax.experimental import pallas as pl
from jax.experimental.pallas import tpu as pltpu

F32 = jnp.float32
BF16 = jnp.bfloat16

N_HEADS = 4
QK_DIM = 64
V_DIM = 128
ATT_W = N_HEADS * V_DIM
Q_W = N_HEADS * 2 * QK_DIM
CONV_K = 31
N_GROUPS = 4
EXPERTS_PER_GROUP = 4
N_EXPERTS = N_GROUPS * EXPERTS_PER_GROUP
EPS = 1e-6
ATT_SCALE = 1.0 / math.sqrt(QK_DIM)
PAGE = 128
NEG = -1e30
LOG2E = math.log2(math.e)
ALIBI_SLOPES = tuple(2.0 ** (-8.0 * (i + 1) / N_HEADS) for i in range(N_HEADS))

LANE = 128
MXU = 256
HALO = 32
VT_ROWS = V_DIM + 16
VMEM_LIMIT = 56 * 1024 * 1024


def _cparams(sem):
    return pltpu.CompilerParams(dimension_semantics=sem, vmem_limit_bytes=VMEM_LIMIT)


def _dot(a, b):
    return jnp.dot(a, b, preferred_element_type=F32)


def _dot_nt(a, b):
    return lax.dot_general(a, b, (((1,), (1,)), ((), ())), preferred_element_type=F32)


def _split_bf16(x):
    hi = x.astype(BF16)
    lo = (x - hi.astype(F32)).astype(BF16)
    return hi, lo


def _sigmoid(x):
    return 1.0 / (1.0 + jnp.exp(-x))


def _adaln_body(c_ref, w_ref, b_ref, o_ref):
    c = c_ref[...]
    o_ref[...] = jnp.dot(c * _sigmoid(c), w_ref[...], preferred_element_type=F32,
                         precision=lax.Precision.HIGHEST) + b_ref[...]


def _adaln(c, w_ada, b_ada):
    nb, d = c.shape
    n_out = w_ada.shape[1]
    bn = 1024
    return pl.pallas_call(
        _adaln_body,
        grid=(n_out // bn,),
        in_specs=[pl.BlockSpec((nb, d), lambda j: (0, 0)),
                  pl.BlockSpec((d, bn), lambda j: (0, j)),
                  pl.BlockSpec((1, bn), lambda j: (0, j))],
        out_specs=pl.BlockSpec((nb, bn), lambda j: (0, j)),
        out_shape=jax.ShapeDtypeStruct((nb, n_out), F32),
        compiler_params=_cparams(("arbitrary",)),
        name="adaln",
    )(c, w_ada, b_ada.reshape(1, n_out))


def _proj_body(x_ref, sh_ref, sc_ref, gmix_ref, w_ref, gq_ref, gk_ref, pm_ref, *out_refs, prompt, tiles_per_seq):
    x = x_ref[...]
    tm = x.shape[0]
    h = x * lax.rsqrt(jnp.mean(x * x, axis=-1, keepdims=True) + EPS) * gmix_ref[...]
    h = h * (1.0 + sc_ref[0]) + sh_ref[0]
    hb = h.astype(BF16)
    pm = pm_ref[...]

    def chunk_rms(z, g):
        outs = []
        for c in range(Q_W // MXU):
            zc = z[:, c * MXU:(c + 1) * MXU]
            hi, lo = _split_bf16(zc * zc)
            ms = _dot(hi, pm) + _dot(lo, pm)
            outs.append(zc * lax.rsqrt(ms + EPS))
        return jnp.concatenate(outs, axis=-1) * g

    qn = chunk_rms(_dot(hb, w_ref[:, 0:Q_W]), gq_ref[...])
    kn = chunk_rms(_dot(hb, w_ref[:, Q_W:2 * Q_W]), gk_ref[...])
    v = _dot(hb, w_ref[:, 2 * Q_W:2 * Q_W + ATT_W])
    a = _dot(hb, w_ref[:, 2 * Q_W + ATT_W:])
    cw = a.shape[-1] // 2
    u = a[:, :cw] * _sigmoid(a[:, cw:])
    if not prompt:
        q_ref, k_ref, v_ref, u_ref = out_refs
        q_ref[...] = qn
        k_ref[...] = kn
        v_ref[...] = v
        u_ref[...] = u
        return

    kt_ref, v4_ref, u_ref, qa_ref, ka_ref, vt_ref = out_refs
    u_ref[...] = u
    kt_ref[0] = kn.T
    for hh in range(N_HEADS):
        v4_ref[pl.ds(hh, tm, stride=N_HEADS), :] = v[:, hh * V_DIM:(hh + 1) * V_DIM]
    vt = v.T
    for hh in range(N_HEADS):
        vt_ref[0, 0, hh, 0:V_DIM, :] = vt[hh * V_DIM:(hh + 1) * V_DIM, :].astype(BF16)
        vt_ref[0, 0, hh, V_DIM:VT_ROWS, :] = jnp.ones((VT_ROWS - V_DIM, tm), BF16)
    lane = lax.broadcasted_iota(jnp.int32, (tm, LANE), 1)
    pos = ((pl.program_id(0) % tiles_per_seq) * tm + lax.broadcasted_iota(jnp.int32, (tm, 1), 0)).astype(F32)
    q_aux = jnp.where(lane < QK_DIM + 3, 1.0, 0.0)
    qs = qn * (ATT_SCALE * LOG2E)
    for hh in range(N_HEADS):
        b = pos * (ALIBI_SLOPES[hh] * LOG2E)
        b_hi = b.astype(BF16).astype(F32)
        b_mid = (b - b_hi).astype(BF16).astype(F32)
        b_lo = b - b_hi - b_mid
        k_aux = jnp.where(lane == QK_DIM, b_hi,
                          jnp.where(lane == QK_DIM + 1, b_mid, jnp.where(lane == QK_DIM + 2, b_lo, 0.0)))
        qg = qs[:, hh * LANE:(hh + 1) * LANE]
        kg = kn[:, hh * LANE:(hh + 1) * LANE]
        for mp in range(2):
            if mp == 1:
                qg = pltpu.roll(qg, QK_DIM, 1)
                kg = pltpu.roll(kg, QK_DIM, 1)
            c0 = (2 * hh + mp) * LANE
            qa_ref[:, c0:c0 + LANE] = jnp.where(lane < QK_DIM, qg, q_aux).astype(BF16)
            ka_ref[:, c0:c0 + LANE] = jnp.where(lane < QK_DIM, kg, k_aux).astype(BF16)


def _proj(x2, sh, sc, gmix, w_in_b, gq_t, gk_t, pm, *, tm, tiles_per_seq, prompt):
    n, d = x2.shape
    nt = n // tm
    in_w = w_in_b.shape[1]
    conv_ch = (in_w - 2 * Q_W - ATT_W) // 2
    mod_rows = sh.shape[1]
    row = lambda i: (i, 0)
    const = lambda i: (0, 0)
    mod_idx = lambda i: (i // tiles_per_seq, 0, 0)
    in_specs = [pl.BlockSpec((tm, d), row),
                pl.BlockSpec((1, mod_rows, d), mod_idx),
                pl.BlockSpec((1, mod_rows, d), mod_idx),
                pl.BlockSpec((1, d), const),
                pl.BlockSpec((d, in_w), const),
                pl.BlockSpec((1, Q_W), const),
                pl.BlockSpec((1, Q_W), const),
                pl.BlockSpec((MXU, MXU), const)]
    if prompt:
        nb = nt // tiles_per_seq
        out_shape = [jax.ShapeDtypeStruct((nb, Q_W, tiles_per_seq * tm), F32),
                     jax.ShapeDtypeStruct((n * N_HEADS, V_DIM), F32),
                     jax.ShapeDtypeStruct((n, conv_ch), F32),
                     jax.ShapeDtypeStruct((n, 2 * Q_W), BF16),
                     jax.ShapeDtypeStruct((n, 2 * Q_W), BF16),
                     jax.ShapeDtypeStruct((nb, tiles_per_seq, N_HEADS, VT_ROWS, tm), BF16)]
        out_specs = [pl.BlockSpec((1, Q_W, tm), lambda i: (i // tiles_per_seq, 0, i % tiles_per_seq)),
                     pl.BlockSpec((tm * N_HEADS, V_DIM), row),
                     pl.BlockSpec((tm, conv_ch), row),
                     pl.BlockSpec((tm, 2 * Q_W), row),
                     pl.BlockSpec((tm, 2 * Q_W), row),
                     pl.BlockSpec((1, 1, N_HEADS, VT_ROWS, tm),
                                  lambda i: (i // tiles_per_seq, i % tiles_per_seq, 0, 0, 0))]
    else:
        f32_out = jax.ShapeDtypeStruct((n, Q_W), F32)
        f32_spec = pl.BlockSpec((tm, Q_W), row)
        out_shape = [f32_out, f32_out, f32_out, jax.ShapeDtypeStruct((n, conv_ch), F32)]
        out_specs = [f32_spec, f32_spec, f32_spec, pl.BlockSpec((tm, conv_ch), row)]
    return pl.pallas_call(
        functools.partial(_proj_body, prompt=prompt, tiles_per_seq=tiles_per_seq),
        grid=(nt,),
        in_specs=in_specs,
        out_specs=out_specs,
        out_shape=out_shape,
        compiler_params=_cparams(("arbitrary",)),
        name="proj_prompt" if prompt else "proj_sample",
    )(x2, sh, sc, gmix, w_in_b, gq_t, gk_t, pm)


def _lambda(lamv, lam_init):
    a = jnp.sum(lamv[0:1] * lamv[1:2], axis=-1, keepdims=True)
    b = jnp.sum(lamv[2:3] * lamv[3:4], axis=-1, keepdims=True)
    return jnp.exp(a) - jnp.exp(b) + lam_init


def _pattn_body(q_ref, k_ref, vt_ref, lamv_ref, gatt_ref, o_ref, m_ref, acc_ref, s0_ref, s1_ref, mb0_ref, mb1_ref,
                *, t, lam_init):
    i = pl.program_id(2)
    m_ref[...] = jnp.full(m_ref.shape, NEG, F32)
    acc_ref[...] = jnp.zeros(acc_ref.shape, F32)

    def scores(j, s_ref, mb_ref):
        kblk = k_ref[0, pl.ds(pl.multiple_of(j * t, t), t), :]
        for mp in range(2):
            sl = slice(mp * t, (mp + 1) * t)
            s = _dot_nt(kblk[:, mp * LANE:(mp + 1) * LANE], q_ref[0, :, mp * LANE:(mp + 1) * LANE])
            s_ref[:, sl] = s
            mb_ref[:, sl] = jnp.max(s, axis=0, keepdims=True)

    def accumulate(j, s_ref, mb_ref, masked):
        vt = vt_ref[0, j, 0]
        for mp in range(2):
            sl = slice(mp * t, (mp + 1) * t)
            s = s_ref[:, sl]
            if masked:
                r = lax.broadcasted_iota(jnp.int32, s.shape, 0)
                c = lax.broadcasted_iota(jnp.int32, s.shape, 1)
                s = jnp.where(r <= c, s, NEG)
                mb = jnp.max(s, axis=0, keepdims=True)
            else:
                mb = mb_ref[:, sl]
            m_old = m_ref[:, sl]
            m_new = jnp.maximum(m_old, mb)
            alpha = jnp.exp2(m_old - m_new)
            p = jnp.exp2(s - m_new).astype(BF16)
            acc_ref[:, sl] = alpha * acc_ref[:, sl] + _dot(vt, p)
            m_ref[:, sl] = m_new

    scores(0, s0_ref, mb0_ref)

    def body(jj, carry):
        j = 2 * jj
        scores(j + 1, s1_ref, mb1_ref)
        accumulate(j, s0_ref, mb0_ref, False)
        scores(j + 2, s0_ref, mb0_ref)
        accumulate(j + 1, s1_ref, mb1_ref, False)
        return carry

    lax.fori_loop(0, i // 2, body, 0)

    @pl.when(i % 2 == 0)
    def _():
        accumulate(i, s0_ref, mb0_ref, True)

    @pl.when(i % 2 == 1)
    def _():
        scores(i, s1_ref, mb1_ref)
        accumulate(i - 1, s0_ref, mb0_ref, False)
        accumulate(i, s1_ref, mb1_ref, True)

    acc = acc_ref[...]
    o = acc[0:V_DIM] * (1.0 / acc[V_DIM:V_DIM + 1])
    lam = _lambda(lamv_ref[...], lam_init)
    d = o[:, :t] - lam * o[:, t:]
    ms = jnp.mean(d * d, axis=0, keepdims=True)
    out = d * lax.rsqrt(ms + EPS) * (gatt_ref[...] * (1.0 - lam_init))
    o_ref[0] = out.T.astype(BF16)


def _prompt_attention(qa, ka, vt, lamv, gatt_col, *, t, lam_init):
    b, s, _ = qa.shape
    nq = s // t
    return pl.pallas_call(
        functools.partial(_pattn_body, t=t, lam_init=lam_init),
        grid=(b, N_HEADS, nq),
        in_specs=[pl.BlockSpec((1, t, 2 * LANE), lambda bb, h, i: (bb, i, h)),
                  pl.BlockSpec((1, s, 2 * LANE), lambda bb, h, i: (bb, 0, h)),
                  pl.BlockSpec((1, nq, 1, VT_ROWS, t), lambda bb, h, i: (bb, 0, h, 0, 0)),
                  pl.BlockSpec((4, QK_DIM), lambda bb, h, i: (0, 0)),
                  pl.BlockSpec((V_DIM, 1), lambda bb, h, i: (0, 0))],
        out_specs=pl.BlockSpec((1, t, LANE), lambda bb, h, i: (bb, i, h)),
        scratch_shapes=[pltpu.VMEM((1, 2 * t), F32),
                        pltpu.VMEM((VT_ROWS, 2 * t), F32),
                        pltpu.VMEM((t, 2 * t), F32),
                        pltpu.VMEM((t, 2 * t), F32),
                        pltpu.VMEM((1, 2 * t), F32),
                        pltpu.VMEM((1, 2 * t), F32)],
        out_shape=jax.ShapeDtypeStruct((b, s, ATT_W), BF16),
        compiler_params=_cparams(("arbitrary", "arbitrary", "arbitrary")),
        name="prompt_attention",
    )(qa, ka, vt, lamv, gatt_col)


def _sattn_body(pt_ref, wq_ref, knew_ref, vnew_ref, slope_ref, qpos_ref, lamv_ref, gatt_ref,
                ck_hbm, cv_hbm, o_ref, kbuf, vbuf, sem, m_ref, l_ref, acc_ref, pad_ref,
                *, cp, nc, layer, t_new, lam_init):
    b = pl.program_id(0)
    c = pl.program_id(1)
    g = b * nc + c
    total = pl.num_programs(0) * nc
    slot = g % 2
    keys = cp * PAGE
    hrows = 2 * t_new

    def copies(bb, cc, sl):
        out = []
        for p in range(cp):
            page = pt_ref[bb, cc * cp + p]
            out.append(pltpu.make_async_copy(ck_hbm.at[layer, page], kbuf.at[sl, :, pl.ds(p * PAGE, PAGE)],
                                             sem.at[0, sl]))
            out.append(pltpu.make_async_copy(cv_hbm.at[layer, page],
                                             vbuf.at[sl, pl.ds(p * PAGE * N_HEADS, PAGE * N_HEADS), :],
                                             sem.at[1, sl]))
        return out

    @pl.when(g == 0)
    def _():
        for cpy in copies(b, c, slot):
            cpy.start()

    @pl.when(g + 1 < total)
    def _():
        gn = g + 1
        for cpy in copies(gn // nc, gn % nc, 1 - slot):
            cpy.start()

    @pl.when(c == 0)
    def _():
        m_ref[...] = jnp.full(m_ref.shape, NEG, F32)
        l_ref[...] = jnp.zeros(l_ref.shape, F32)
        acc_ref[...] = jnp.zeros(acc_ref.shape, F32)

    for cpy in copies(b, c, slot):
        cpy.wait()

    wq = wq_ref[0]
    slope = slope_ref[...]
    qpos = qpos_ref[...]

    def update(s, head_values):
        m_old = m_ref[...]
        m_new = jnp.maximum(m_old, jnp.max(s, axis=-1, keepdims=True))
        alpha = jnp.exp(m_old - m_new)
        p = jnp.exp(s - m_new)
        l_ref[...] = alpha * l_ref[...] + jnp.sum(p, axis=-1, keepdims=True)
        for hh in range(N_HEADS):
            rs = slice(hh * hrows, (hh + 1) * hrows)
            acc_ref[rs, :] = alpha[rs] * acc_ref[rs, :] + _dot(p[rs], head_values(hh))
        m_ref[...] = m_new

    kpos = (c * keys + lax.broadcasted_iota(jnp.int32, (1, keys), 1)).astype(F32)
    s = _dot(wq, kbuf[slot]) - slope * (qpos - kpos)
    vslot = vbuf.at[slot]
    update(s, lambda hh: vslot[pl.ds(hh, keys, stride=N_HEADS), :])

    @pl.when(c == nc - 1)
    def _():
        pad_ref[...] = jnp.zeros(pad_ref.shape, F32)
        pad_ref[0:t_new, :] = knew_ref[0]
        s2 = _dot_nt(wq, pad_ref[...])
        tk = lax.broadcasted_iota(jnp.int32, s2.shape, 1)
        tq = lax.broadcasted_iota(jnp.int32, s2.shape, 0) % t_new
        s2 = jnp.where(tk <= tq, s2 - slope * (tq - tk).astype(F32), NEG)
        pad_ref[0:t_new, :] = vnew_ref[0]
        update(s2, lambda hh: pad_ref[:, hh * V_DIM:(hh + 1) * V_DIM])

        o = acc_ref[...] * (1.0 / l_ref[...])
        lam = _lambda(lamv_ref[...], lam_init)
        gatt = gatt_ref[...] * (1.0 - lam_init)
        for hh in range(N_HEADS):
            r0 = hh * hrows
            d = o[r0:r0 + t_new] - lam * o[r0 + t_new:r0 + hrows]
            ms = jnp.mean(d * d, axis=-1, keepdims=True)
            o_ref[0, :, hh * V_DIM:(hh + 1) * V_DIM] = d * lax.rsqrt(ms + EPS) * gatt


def _sample_attention(page_table, wq, k_new, v_new, slope_col, qpos_col, lamv, gatt_row,
                      cache_kt, cache_v4, *, layer, lam_init):
    db, t_new, _ = k_new.shape
    n_pages = page_table.shape[1]
    cp = math.gcd(n_pages, 16)
    nc = n_pages // cp
    rows_q = wq.shape[1]
    const2 = lambda b, c, pt: (0, 0)
    per_b = lambda b, c, pt: (b, 0, 0)
    return pl.pallas_call(
        functools.partial(_sattn_body, cp=cp, nc=nc, layer=layer, t_new=t_new, lam_init=lam_init),
        grid_spec=pltpu.PrefetchScalarGridSpec(
            num_scalar_prefetch=1,
            grid=(db, nc),
            in_specs=[pl.BlockSpec((1, rows_q, Q_W), per_b),
                      pl.BlockSpec((1, t_new, Q_W), per_b),
                      pl.BlockSpec((1, t_new, ATT_W), per_b),
                      pl.BlockSpec((rows_q, 1), const2),
                      pl.BlockSpec((rows_q, 1), const2),
                      pl.BlockSpec((4, QK_DIM), const2),
                      pl.BlockSpec((1, V_DIM), const2),
                      pl.BlockSpec(memory_space=pl.ANY),
                      pl.BlockSpec(memory_space=pl.ANY)],
            out_specs=pl.BlockSpec((1, t_new, ATT_W), per_b),
            scratch_shapes=[pltpu.VMEM((2, Q_W, cp * PAGE), F32),
                            pltpu.VMEM((2, cp * PAGE * N_HEADS, V_DIM), F32),
                            pltpu.SemaphoreType.DMA((2, 2)),
                            pltpu.VMEM((rows_q, 1), F32),
                            pltpu.VMEM((rows_q, 1), F32),
                            pltpu.VMEM((rows_q, V_DIM), F32),
                            pltpu.VMEM((PAGE, Q_W), F32)]),
        out_shape=jax.ShapeDtypeStruct((db, t_new, ATT_W), F32),
        compiler_params=_cparams(("arbitrary", "arbitrary")),
        name="sample_attention",
    )(page_table, wq, k_new, v_new, slope_col, qpos_col, lamv, gatt_row, cache_kt, cache_v4)


def _post_body(x_ref, att_ref, u_ref, halo_ref, gt1_ref, sh2_ref, sc2_ref, wdw_ref, bdw_ref, gln_ref, bln_ref,
               wpw_ref, wout_ref, gffn_ref, wr_ref, br_ref, x1_ref, h2_ref, lg_ref, useq_ref, aux_ref,
               *, tm, seq_rows):
    wdw = wdw_ref[...]
    off = HALO - (CONV_K - 1)

    def taps(rows, read):
        acc = jnp.zeros((rows, wdw.shape[1]), F32) + bdw_ref[...]
        for j in range(CONV_K):
            acc = acc + wdw[j:j + 1, :] * read(off + j)
        return acc

    if seq_rows is None:
        halo = halo_ref[0]
        halo = jnp.where(pl.program_id(1) > 0, halo, jnp.zeros_like(halo))
        useq_ref[0:HALO, :] = halo
        useq_ref[HALO:HALO + tm, :] = u_ref[0]
        rows = tm + HALO - 8
        for s in range(1, 8):
            aux_ref[s - 1, 0:rows, :] = useq_ref[s:s + rows, :]

        def read(o):
            s = o % 8
            src = useq_ref if s == 0 else aux_ref.at[s - 1]
            return src[o - s:o - s + tm, :]

        acc = taps(tm, read)
    else:
        def one_seq(b, carry):
            useq_ref[0:HALO, :] = halo_ref[b]
            useq_ref[HALO:HALO + seq_rows, :] = u_ref[b]
            aux_ref[pl.ds(pl.multiple_of(b * seq_rows, seq_rows), seq_rows), :] = taps(
                seq_rows, lambda o: useq_ref[o:o + seq_rows, :])
            return carry

        lax.fori_loop(0, tm // seq_rows, one_seq, 0)
        acc = aux_ref[...]
    mu = jnp.mean(acc, axis=-1, keepdims=True)
    cen = acc - mu
    var = jnp.mean(cen * cen, axis=-1, keepdims=True)
    y = cen * lax.rsqrt(var + EPS) * gln_ref[...] + bln_ref[...]
    y = y * _sigmoid(y)
    conv = _dot(y.astype(BF16), wpw_ref[...])
    att = att_ref[0].astype(BF16)
    merged = _dot(att, wout_ref[0:ATT_W, :]) + _dot(conv.astype(BF16), wout_ref[ATT_W:, :])
    x1 = x_ref[0] + gt1_ref[0] * merged
    x1_ref[0] = x1
    h2 = x1 * lax.rsqrt(jnp.mean(x1 * x1, axis=-1, keepdims=True) + EPS) * gffn_ref[...]
    h2 = h2 * (1.0 + sc2_ref[0]) + sh2_ref[0]
    hi, lo = _split_bf16(h2)
    h2_ref[0] = hi
    z = _dot(hi, wr_ref[...]) + _dot(lo, wr_ref[...])
    lg_ref[0] = z[:, :LANE] + z[:, LANE:] + br_ref[...]


def _post(x3, att3, u3, halo_src, gt1, sh2, sc2, wdw, bdw, gln, bln, wpw_b, wout_b, gffn, wr2, br,
          *, tm, seq_rows):
    nb, s, d = x3.shape
    nt = s // tm
    ch = u3.shape[-1]
    tile = lambda b, i: (b, i, 0)
    mod = lambda b, i: (b, 0, 0)
    const = lambda b, i: (0, 0)
    mod_rows = gt1.shape[1]
    if seq_rows is None:
        u_spec = pl.BlockSpec((1, tm, ch), tile)
        halo_spec = pl.BlockSpec((1, HALO, ch), lambda b, i: (b, jnp.maximum(i * (tm // HALO) - 1, 0), 0))
        scratch = [pltpu.VMEM((HALO + tm, ch), F32), pltpu.VMEM((7, HALO + tm, ch), F32)]
    else:
        u_spec = pl.BlockSpec(u3.shape, lambda b, i: (0, 0, 0))
        halo_spec = pl.BlockSpec(halo_src.shape, lambda b, i: (0, 0, 0))
        scratch = [pltpu.VMEM((HALO + seq_rows, ch), F32), pltpu.VMEM((tm, ch), F32)]
    return pl.pallas_call(
        functools.partial(_post_body, tm=tm, seq_rows=seq_rows),
        grid=(nb, nt),
        in_specs=[pl.BlockSpec((1, tm, d), tile),
                  pl.BlockSpec((1, tm, ATT_W), tile),
                  u_spec,
                  halo_spec,
                  pl.BlockSpec((1, mod_rows, d), mod),
                  pl.BlockSpec((1, mod_rows, d), mod),
                  pl.BlockSpec((1, mod_rows, d), mod),
                  pl.BlockSpec((CONV_K, ch), const),
                  pl.BlockSpec((1, ch), const),
                  pl.BlockSpec((1, ch), const),
                  pl.BlockSpec((1, ch), const),
                  pl.BlockSpec((ch, ch), const),
                  pl.BlockSpec((d, d), const),
                  pl.BlockSpec((1, d), const),
                  pl.BlockSpec((d, 2 * LANE), const),
                  pl.BlockSpec((1, LANE), const)],
        out_specs=[pl.BlockSpec((1, tm, d), tile),
                   pl.BlockSpec((1, tm, d), tile),
                   pl.BlockSpec((1, tm, LANE), tile)],
        out_shape=[jax.ShapeDtypeStruct((nb, s, d), F32),
                   jax.ShapeDtypeStruct((nb, s, d), BF16),
                   jax.ShapeDtypeStruct((nb, s, LANE), F32)],
        scratch_shapes=scratch,
        compiler_params=_cparams(("arbitrary", "arbitrary")),
        name="post_prompt" if seq_rows is None else "post_sample",
    )(x3, att3, u3, halo_src, gt1, sh2, sc2, wdw, bdw, gln, bln, wpw_b, wout_b, gffn, wr2, br)


def _combine_weights(lg):
    lane = lax.broadcasted_iota(jnp.int32, lg.shape, 1).astype(F32)
    big = 1e4
    gl = jnp.where(lane < N_GROUPS, lg, NEG)
    gmax = jnp.max(gl, axis=-1, keepdims=True)
    gsum = jnp.sum(jnp.exp(gl - gmax), axis=-1, keepdims=True)
    g_val = 1.0 / gsum
    g_idx = jnp.min(jnp.where(gl == gmax, lane, big), axis=-1, keepdims=True)
    lo = N_GROUPS + EXPERTS_PER_GROUP * g_idx
    el = jnp.where(lane >= lo, jnp.where(lane < lo + EXPERTS_PER_GROUP, lg, NEG), NEG)
    v1 = jnp.max(el, axis=-1, keepdims=True)
    i1 = jnp.min(jnp.where(el == v1, lane, big), axis=-1, keepdims=True)
    el2 = jnp.where(lane == i1, NEG, el)
    v2 = jnp.max(el2, axis=-1, keepdims=True)
    i2 = jnp.min(jnp.where(el2 == v2, lane, big), axis=-1, keepdims=True)
    e21 = jnp.exp(v2 - v1)
    w1 = g_val / (1.0 + e21)
    w2 = w1 * e21
    return jnp.where(lane == i1, w1, 0.0) + jnp.where(lane == i2, w2, 0.0), g_idx


def _dot_tn(a, b):
    return lax.dot_general(a, b, (((0,), (0,)), ((), ())), preferred_element_type=F32)


def _moe_body(x1_ref, h2_ref, lg_ref, gt2_ref, wg_ref, wu_ref, wd_ref, o_ref, xs_ref, cs_ref, ys_ref, off_ref, *, bm):
    tm = h2_ref.shape[1]
    comb, g_idx = _combine_weights(lg_ref[0])
    lane = lax.broadcasted_iota(jnp.int32, (tm, LANE), 1).astype(F32)
    onehot = jnp.where(lane == g_idx, 1.0, 0.0)
    r_i = lax.broadcasted_iota(jnp.int32, (tm, tm), 0)
    c_i = lax.broadcasted_iota(jnp.int32, (tm, tm), 1)
    earlier = _dot(jnp.where(c_i < r_i, 1.0, 0.0).astype(BF16), onehot.astype(BF16))
    totals = jnp.sum(onehot, axis=0, keepdims=True)
    lane1 = lane[0:1]
    starts = jnp.zeros((1, LANE), F32)
    run = jnp.zeros((1, 1), F32)
    off_ref[0] = 0
    for g in range(N_GROUPS):
        starts = starts + jnp.where(lane1 == g, run, 0.0)
        run = run + jnp.sum(jnp.where(lane1 == g, totals, 0.0), axis=-1, keepdims=True)
        off_ref[g + 1] = jnp.sum(run).astype(jnp.int32)
    pos = jnp.sum(onehot * (earlier + starts), axis=-1, keepdims=True)
    perm = jnp.where(pos == c_i.astype(F32), 1.0, 0.0).astype(BF16)
    xs_ref[...] = _dot_tn(perm, h2_ref[0]).astype(BF16)
    c_hi, c_lo = _split_bf16(comb)
    cs_ref[...] = _dot_tn(perm, c_hi) + _dot_tn(perm, c_lo)
    ys_ref[...] = jnp.zeros(ys_ref.shape, F32)

    def block(k, carry):
        r0 = pl.multiple_of(k * bm, bm)

        def group(g, carry2):
            @pl.when((off_ref[g] < r0 + bm) & (off_ref[g + 1] > r0))
            def _():
                xb = xs_ref[pl.ds(r0, bm), :]
                cb = cs_ref[pl.ds(r0, bm), :]
                lane_b = lax.broadcasted_iota(jnp.int32, cb.shape, 1)
                yb = jnp.zeros((bm, ys_ref.shape[1]), F32)
                for ee in range(EXPERTS_PER_GROUP):
                    e = g * EXPERTS_PER_GROUP + ee
                    gate = _dot(xb, wg_ref[e])
                    up = _dot(xb, wu_ref[e])
                    ce = jnp.sum(jnp.where(lane_b == N_GROUPS + e, cb, 0.0), axis=-1, keepdims=True)
                    yb = yb + _dot((gate * _sigmoid(gate) * up * ce).astype(BF16), wd_ref[e])
                ys_ref[pl.ds(r0, bm), :] = ys_ref[pl.ds(r0, bm), :] + yb
            return carry2

        return lax.fori_loop(0, N_GROUPS, group, carry)

    lax.fori_loop(0, tm // bm, block, 0)
    y = _dot(perm, ys_ref[...].astype(BF16))
    o_ref[0] = x1_ref[0] + gt2_ref[0] * y


def _moe(x1, h2, lg, gt2, wg_b, wu_b, wd_b, *, tm):
    nb, s, d = x1.shape
    nt = s // tm
    mod_rows = gt2.shape[1]
    tile = lambda b, i: (b, i, 0)
    const3 = lambda b, i: (0, 0, 0)
    single = pl.Buffered(1)
    bm = min(tm, LANE)
    return pl.pallas_call(
        functools.partial(_moe_body, bm=bm),
        grid=(nb, nt),
        in_specs=[pl.BlockSpec((1, tm, d), tile),
                  pl.BlockSpec((1, tm, d), tile),
                  pl.BlockSpec((1, tm, LANE), tile),
                  pl.BlockSpec((1, mod_rows, d), lambda b, i: (b, 0, 0)),
                  pl.BlockSpec(wg_b.shape, const3, pipeline_mode=single),
                  pl.BlockSpec(wu_b.shape, const3, pipeline_mode=single),
                  pl.BlockSpec(wd_b.shape, const3, pipeline_mode=single)],
        out_specs=pl.BlockSpec((1, tm, d), tile),
        out_shape=jax.ShapeDtypeStruct((nb, s, d), F32),
        scratch_shapes=[pltpu.VMEM((tm, d), BF16),
                        pltpu.VMEM((tm, LANE), F32),
                        pltpu.VMEM((tm, d), F32),
                        pltpu.SMEM((N_GROUPS + 1,), jnp.int32)],
        compiler_params=_cparams(("arbitrary", "arbitrary")),
        name="moe",
    )(x1, h2, lg, gt2, wg_b, wu_b, wd_b)


def kernel(x_prompt, x_sample, c_prompt, c_sample, cache_k, cache_v, state_conv, page_table, w_ada, b_ada, g_norm_mix, w_in, g_q, g_k, lambda_q1, lambda_k1, lambda_q2, lambda_k2, g_attn_out, w_dw, b_dw, g_conv_ln, b_conv_ln, w_conv_pw, w_out, g_norm_ffn, w_router_group, b_router_group, w_router_expert, b_router_expert, w_exp_gate, w_exp_up, w_exp_down):
    depth = w_in.shape[0]
    bsz, seq, d = x_prompt.shape
    db, t_new, _ = x_sample.shape
    n_pages = page_table.shape[1]
    past = n_pages * PAGE
    n_phys = cache_k.shape[1]
    conv_ch = w_dw.shape[-1]
    t_att = min(512, seq)
    tm_post = min(256, seq)
    tm_moe = min(512, seq)
    n_s = db * t_new

    blk = jnp.arange(MXU) // QK_DIM
    pm = jnp.where(blk[:, None] == blk[None, :], 1.0 / QK_DIM, 0.0).astype(BF16)
    rows_q = 2 * N_HEADS * t_new
    row_head = jnp.arange(rows_q) // (2 * t_new)
    slope_col = jnp.asarray(ALIBI_SLOPES, F32)[row_head].reshape(rows_q, 1)
    qpos_col = (past + jnp.arange(rows_q) % t_new).astype(F32).reshape(rows_q, 1)
    cache_kt = jnp.transpose(cache_k, (0, 1, 3, 4, 5, 2)).reshape(depth, n_phys, Q_W, PAGE)
    cache_v4 = cache_v.reshape(depth, n_phys, PAGE * N_HEADS, V_DIM)

    yp = x_prompt
    ys = x_sample
    kp_l, vp_l, cp_l, ks_l, vs_l, cs_l = [], [], [], [], [], []
    for l in range(depth):
        lam_init = 0.8 - 0.6 * math.exp(-0.3 * l)
        lamv = jnp.stack([lambda_q1[l], lambda_k1[l], lambda_q2[l], lambda_k2[l]]).astype(F32)
        w_in_b = w_in[l].astype(BF16)
        wpw_b = w_conv_pw[l].astype(BF16)
        wout_b = w_out[l].astype(BF16)
        wg_b = w_exp_gate[l].astype(BF16)
        wu_b = w_exp_up[l].astype(BF16)
        wd_b = w_exp_down[l].astype(BF16)
        gq_t = jnp.tile(g_q[l], 2 * N_HEADS).reshape(1, Q_W)
        gk_t = jnp.tile(g_k[l], 2 * N_HEADS).reshape(1, Q_W)
        gmix = g_norm_mix[l].reshape(1, d)
        gffn = g_norm_ffn[l].reshape(1, d)
        wr = jnp.zeros((d, LANE), F32)
        wr = wr.at[:, :N_GROUPS].set(w_router_group[l]).at[:, N_GROUPS:N_GROUPS + N_EXPERTS].set(w_router_expert[l])
        wr_hi = wr.astype(BF16)
        wr2 = jnp.concatenate([wr_hi, (wr - wr_hi.astype(F32)).astype(BF16)], axis=1)
        br = jnp.zeros((1, LANE), F32)
        br = br.at[0, :N_GROUPS].set(b_router_group[l]).at[0, N_GROUPS:N_GROUPS + N_EXPERTS].set(b_router_expert[l])
        conv_w = (w_dw[l], b_dw[l].reshape(1, conv_ch), g_conv_ln[l].reshape(1, conv_ch),
                  b_conv_ln[l].reshape(1, conv_ch), wpw_b, wout_b, gffn, wr2, br)

        mod = _adaln(jnp.concatenate([c_prompt, c_sample], axis=0), w_ada[l], b_ada[l])
        mod = mod.reshape(bsz + db, 6, 1, d)
        modp = [mod[:bsz, j] for j in range(6)]
        mods = [mod[bsz:, j] for j in range(6)]

        kt, v4, up, qa, ka, vt = _proj(
            yp.reshape(bsz * seq, d), modp[0], modp[1], gmix, w_in_b, gq_t, gk_t, pm,
            tm=t_att, tiles_per_seq=seq // t_att, prompt=True)
        att_p = _prompt_attention(
            qa.reshape(bsz, seq, 2 * Q_W), ka.reshape(bsz, seq, 2 * Q_W), vt, lamv,
            g_attn_out[l].reshape(V_DIM, 1), t=t_att, lam_init=lam_init)
        up3 = up.reshape(bsz, seq, conv_ch)
        x1p, h2p, lgp = _post(yp, att_p, up3, up3, modp[2], modp[3], modp[4], *conv_w,
                              tm=tm_post, seq_rows=None)
        yp = _moe(x1p, h2p, lgp, modp[5], wg_b, wu_b, wd_b, tm=tm_moe)

        rep = lambda m: jnp.broadcast_to(m, (db, t_new, d)).reshape(1, n_s, d)
        qsm, ksm, vsm, usm = _proj(
            ys.reshape(n_s, d), rep(mods[0]), rep(mods[1]), gmix, w_in_b, gq_t, gk_t, pm,
            tm=n_s, tiles_per_seq=1, prompt=False)
        q5 = (qsm * ATT_SCALE).reshape(db, t_new, 2 * N_HEADS, QK_DIM)
        eye = jnp.eye(2 * N_HEADS, dtype=F32)
        wq = (q5.transpose(0, 2, 1, 3)[:, :, :, None, :] * eye[None, :, None, :, None])
        wq = wq.reshape(db, rows_q, Q_W)
        att_s = _sample_attention(
            page_table, wq, ksm.reshape(db, t_new, Q_W), vsm.reshape(db, t_new, ATT_W), slope_col, qpos_col,
            lamv, g_attn_out[l].reshape(1, V_DIM), cache_kt, cache_v4, layer=l, lam_init=lam_init)
        us3 = usm.reshape(db, t_new, conv_ch)
        state_pad = jnp.pad(state_conv[l], ((0, 0), (HALO - (CONV_K - 1), 0), (0, 0)))
        x1s, h2s, lgs = _post(ys.reshape(1, n_s, d), att_s.reshape(1, n_s, ATT_W), us3, state_pad,
                              rep(mods[2]), rep(mods[3]), rep(mods[4]), *conv_w, tm=n_s, seq_rows=t_new)
        ys = _moe(x1s, h2s, lgs, rep(mods[5]), wg_b, wu_b, wd_b, tm=n_s).reshape(db, t_new, d)

        kp_l.append(jnp.transpose(kt.reshape(bsz, N_HEADS, 2, QK_DIM, seq), (0, 4, 1, 2, 3)))
        vp_l.append(v4.reshape(bsz, seq, N_HEADS, V_DIM))
        cp_l.append(up3[:, seq - (CONV_K - 1):])
        ks_l.append(ksm.reshape(db, t_new, N_HEADS, 2, QK_DIM))
        vs_l.append(vsm.reshape(db, t_new, N_HEADS, V_DIM))
        cs_l.append(jnp.concatenate([state_conv[l], us3], axis=1)[:, t_new:])

    return (yp, ys, jnp.stack(kp_l), jnp.stack(vp_l), jnp.stack(cp_l),
            jnp.stack(ks_l), jnp.stack(vs_l), jnp.stack(cs_l))
```

```python
import functools
import math

import jax
import jax.numpy as jnp
from jax import lax
from jax.experimental import pallas as pl
from jax.experimental.pallas import tpu as pltpu

F32 = jnp.float32
BF16 = jnp.bfloat16

N_HEADS = 4
QK_DIM = 64
V_DIM = 128
ATT_W = N_HEADS * V_DIM
Q_W = N_HEADS * 2 * QK_DIM
CONV_K = 31
N_GROUPS = 4
EXPERTS_PER_GROUP = 4
N_EXPERTS = N_GROUPS * EXPERTS_PER_GROUP
EPS = 1e-6
ATT_SCALE = 1.0 / math.sqrt(QK_DIM)
PAGE = 128
NEG = -1e30
LOG2E = math.log2(math.e)
ALIBI_SLOPES = tuple(2.0 ** (-8.0 * (i + 1) / N_HEADS) for i in range(N_HEADS))

LANE = 128
MXU = 256
HALO = 32
VT_ROWS = V_DIM + 16
VMEM_LIMIT = 56 * 1024 * 1024


def _cparams(sem):
    return pltpu.CompilerParams(dimension_semantics=sem, vmem_limit_bytes=VMEM_LIMIT)


def _dot(a, b):
    return jnp.dot(a, b, preferred_element_type=F32)


def _dot_nt(a, b):
    return lax.dot_general(a, b, (((1,), (1,)), ((), ())), preferred_element_type=F32)


def _split_bf16(x):
    hi = x.astype(BF16)
    lo = (x - hi.astype(F32)).astype(BF16)
    return hi, lo


def _sigmoid(x):
    return 1.0 / (1.0 + jnp.exp(-x))


def _adaln_body(c_ref, w_ref, b_ref, o_ref):
    c = c_ref[...]
    o_ref[...] = jnp.dot(c * _sigmoid(c), w_ref[...], preferred_element_type=F32,
                         precision=lax.Precision.HIGHEST) + b_ref[...]


def _adaln(c, w_ada, b_ada):
    nb, d = c.shape
    n_out = w_ada.shape[1]
    bn = 1024
    return pl.pallas_call(
        _adaln_body,
        grid=(n_out // bn,),
        in_specs=[pl.BlockSpec((nb, d), lambda j: (0, 0)),
                  pl.BlockSpec((d, bn), lambda j: (0, j)),
                  pl.BlockSpec((1, bn), lambda j: (0, j))],
        out_specs=pl.BlockSpec((nb, bn), lambda j: (0, j)),
        out_shape=jax.ShapeDtypeStruct((nb, n_out), F32),
        compiler_params=_cparams(("arbitrary",)),
        name="adaln",
    )(c, w_ada, b_ada.reshape(1, n_out))


def _proj_body(x_ref, sh_ref, sc_ref, gmix_ref, w_ref, gq_ref, gk_ref, pm_ref, *out_refs, prompt, tiles_per_seq):
    x = x_ref[...]
    tm = x.shape[0]
    h = x * lax.rsqrt(jnp.mean(x * x, axis=-1, keepdims=True) + EPS) * gmix_ref[...]
    h = h * (1.0 + sc_ref[0]) + sh_ref[0]
    hb = h.astype(BF16)
    pm = pm_ref[...]

    def chunk_rms(z, g):
        outs = []
        for c in range(Q_W // MXU):
            zc = z[:, c * MXU:(c + 1) * MXU]
            hi, lo = _split_bf16(zc * zc)
            ms = _dot(hi, pm) + _dot(lo, pm)
            outs.append(zc * lax.rsqrt(ms + EPS))
        return jnp.concatenate(outs, axis=-1) * g

    qn = chunk_rms(_dot(hb, w_ref[:, 0:Q_W]), gq_ref[...])
    kn = chunk_rms(_dot(hb, w_ref[:, Q_W:2 * Q_W]), gk_ref[...])
    v = _dot(hb, w_ref[:, 2 * Q_W:2 * Q_W + ATT_W])
    a = _dot(hb, w_ref[:, 2 * Q_W + ATT_W:])
    cw = a.shape[-1] // 2
    u = a[:, :cw] * _sigmoid(a[:, cw:])
    if not prompt:
        q_ref, k_ref, v_ref, u_ref = out_refs
        q_ref[...] = qn
        k_ref[...] = kn
        v_ref[...] = v
        u_ref[...] = u
        return

    kt_ref, v4_ref, u_ref, qa_ref, ka_ref, vt_ref = out_refs
    u_ref[...] = u
    kt_ref[0] = kn.T
    for hh in range(N_HEADS):
        v4_ref[pl.ds(hh, tm, stride=N_HEADS), :] = v[:, hh * V_DIM:(hh + 1) * V_DIM]
    vt = v.T
    for hh in range(N_HEADS):
        vt_ref[0, 0, hh, 0:V_DIM, :] = vt[hh * V_DIM:(hh + 1) * V_DIM, :].astype(BF16)
        vt_ref[0, 0, hh, V_DIM:VT_ROWS, :] = jnp.ones((VT_ROWS - V_DIM, tm), BF16)
    lane = lax.broadcasted_iota(jnp.int32, (tm, LANE), 1)
    pos = ((pl.program_id(0) % tiles_per_seq) * tm + lax.broadcasted_iota(jnp.int32, (tm, 1), 0)).astype(F32)
    q_aux = jnp.where(lane < QK_DIM + 3, 1.0, 0.0)
    qs = qn * (ATT_SCALE * LOG2E)
    for hh in range(N_HEADS):
        b = pos * (ALIBI_SLOPES[hh] * LOG2E)
        b_hi = b.astype(BF16).astype(F32)
        b_mid = (b - b_hi).astype(BF16).astype(F32)
        b_lo = b - b_hi - b_mid
        k_aux = jnp.where(lane == QK_DIM, b_hi,
                          jnp.where(lane == QK_DIM + 1, b_mid, jnp.where(lane == QK_DIM + 2, b_lo, 0.0)))
        qg = qs[:, hh * LANE:(hh + 1) * LANE]
        kg = kn[:, hh * LANE:(hh + 1) * LANE]
        for mp in range(2):
            if mp == 1:
                qg = pltpu.roll(qg, QK_DIM, 1)
                kg = pltpu.roll(kg, QK_DIM, 1)
            c0 = (2 * hh + mp) * LANE
            qa_ref[:, c0:c0 + LANE] = jnp.where(lane < QK_DIM, qg, q_aux).astype(BF16)
            ka_ref[:, c0:c0 + LANE] = jnp.where(lane < QK_DIM, kg, k_aux).astype(BF16)


def _proj(x2, sh, sc, gmix, w_in_b, gq_t, gk_t, pm, *, tm, tiles_per_seq, prompt):
    n, d = x2.shape
    nt = n // tm
    in_w = w_in_b.shape[1]
    conv_ch = (in_w - 2 * Q_W - ATT_W) // 2
    mod_rows = sh.shape[1]
    row = lambda i: (i, 0)
    const = lambda i: (0, 0)
    mod_idx = lambda i: (i // tiles_per_seq, 0, 0)
    in_specs = [pl.BlockSpec((tm, d), row),
                pl.BlockSpec((1, mod_rows, d), mod_idx),
                pl.BlockSpec((1, mod_rows, d), mod_idx),
                pl.BlockSpec((1, d), const),
                pl.BlockSpec((d, in_w), const),
                pl.BlockSpec((1, Q_W), const),
                pl.BlockSpec((1, Q_W), const),
                pl.BlockSpec((MXU, MXU), const)]
    if prompt:
        nb = nt // tiles_per_seq
        out_shape = [jax.ShapeDtypeStruct((nb, Q_W, tiles_per_seq * tm), F32),
                     jax.ShapeDtypeStruct((n * N_HEADS, V_DIM), F32),
                     jax.ShapeDtypeStruct((n, conv_ch), F32),
                     jax.ShapeDtypeStruct((n, 2 * Q_W), BF16),
                     jax.ShapeDtypeStruct((n, 2 * Q_W), BF16),
                     jax.ShapeDtypeStruct((nb, tiles_per_seq, N_HEADS, VT_ROWS, tm), BF16)]
        out_specs = [pl.BlockSpec((1, Q_W, tm), lambda i: (i // tiles_per_seq, 0, i % tiles_per_seq)),
                     pl.BlockSpec((tm * N_HEADS, V_DIM), row),
                     pl.BlockSpec((tm, conv_ch), row),
                     pl.BlockSpec((tm, 2 * Q_W), row),
                     pl.BlockSpec((tm, 2 * Q_W), row),
                     pl.BlockSpec((1, 1, N_HEADS, VT_ROWS, tm),
                                  lambda i: (i // tiles_per_seq, i % tiles_per_seq, 0, 0, 0))]
    else:
        f32_out = jax.ShapeDtypeStruct((n, Q_W), F32)
        f32_spec = pl.BlockSpec((tm, Q_W), row)
        out_shape = [f32_out, f32_out, f32_out, jax.ShapeDtypeStruct((n, conv_ch), F32)]
        out_specs = [f32_spec, f32_spec, f32_spec, pl.BlockSpec((tm, conv_ch), row)]
    return pl.pallas_call(
        functools.partial(_proj_body, prompt=prompt, tiles_per_seq=tiles_per_seq),
        grid=(nt,),
        in_specs=in_specs,
        out_specs=out_specs,
        out_shape=out_shape,
        compiler_params=_cparams(("arbitrary",)),
        name="proj_prompt" if prompt else "proj_sample",
    )(x2, sh, sc, gmix, w_in_b, gq_t, gk_t, pm)


def _lambda(lamv, lam_init):
    a = jnp.sum(lamv[0:1] * lamv[1:2], axis=-1, keepdims=True)
    b = jnp.sum(lamv[2:3] * lamv[3:4], axis=-1, keepdims=True)
    return jnp.exp(a) - jnp.exp(b) + lam_init


def _pattn_body(*refs, t, lam_init, paged):
    if paged is None:
        (q_ref, k_ref, vt_ref, lamv_ref, gatt_ref, o_ref,
         m_ref, acc_ref, s0_ref, s1_ref, mb0_ref, mb1_ref) = refs

        def page_phase(idx, which):
            pass
    else:
        pt_ref, q_ref, k_ref, vt_ref, lamv_ref, gatt_ref = refs[:6]
        o_ref, os_ref, m_ref, acc_ref, s0_ref, s1_ref, mb0_ref, mb1_ref = refs[15:23]
        static = dict(paged)
        per_slot = static.pop("per_slot")
        paged_exact = static.pop("exact")
        nq = static.pop("nq")
        prime, phases = _paged_chunk_fns(pt_ref, *refs[6:15], os_ref, *refs[23:], lam_init=lam_init, **static)
        first = (pl.program_id(0) == 0) & (pl.program_id(1) == 0) & (pl.program_id(2) == 0)

        @pl.when(first)
        def _():
            prime()

        qi = pl.program_id(2)
        slots_bh = (nq * nq) // 4
        slot0 = (pl.program_id(0) * N_HEADS + pl.program_id(1)) * slots_bh + (qi * qi) // 4

        def page_phase(idx, which):
            if per_slot == 1 and paged_exact:
                phases[which](slot0 + idx)
            elif which == len(phases) - 1:
                for r in range(per_slot):
                    g = (slot0 + idx) * per_slot + r

                    @pl.when(g < static["total"])
                    def _():
                        for phase in phases:
                            phase(g)

    i = pl.program_id(2)
    m_ref[...] = jnp.full(m_ref.shape, NEG, F32)
    acc_ref[...] = jnp.zeros(acc_ref.shape, F32)

    def scores(j, s_ref, mb_ref):
        kblk = k_ref[0, pl.ds(pl.multiple_of(j * t, t), t), :]
        for mp in range(2):
            sl = slice(mp * t, (mp + 1) * t)
            s = _dot_nt(kblk[:, mp * LANE:(mp + 1) * LANE], q_ref[0, :, mp * LANE:(mp + 1) * LANE])
            s_ref[:, sl] = s
            mb_ref[:, sl] = jnp.max(s, axis=0, keepdims=True)

    def accumulate(j, s_ref, mb_ref, masked):
        vt = vt_ref[0, j, 0]
        for mp in range(2):
            sl = slice(mp * t, (mp + 1) * t)
            s = s_ref[:, sl]
            if masked:
                r = lax.broadcasted_iota(jnp.int32, s.shape, 0)
                c = lax.broadcasted_iota(jnp.int32, s.shape, 1)
                s = jnp.where(r <= c, s, NEG)
                mb = jnp.max(s, axis=0, keepdims=True)
            else:
                mb = mb_ref[:, sl]
            m_old = m_ref[:, sl]
            m_new = jnp.maximum(m_old, mb)
            alpha = jnp.exp2(m_old - m_new)
            p = jnp.exp2(s - m_new).astype(BF16)
            acc_ref[:, sl] = alpha * acc_ref[:, sl] + _dot(vt, p)
            m_ref[:, sl] = m_new

    scores(0, s0_ref, mb0_ref)

    def body(jj, carry):
        j = 2 * jj
        page_phase(jj, 0)
        scores(j + 1, s1_ref, mb1_ref)
        page_phase(jj, 1)
        accumulate(j, s0_ref, mb0_ref, False)
        scores(j + 2, s0_ref, mb0_ref)
        accumulate(j + 1, s1_ref, mb1_ref, False)
        page_phase(jj, 2)
        page_phase(jj, 3)
        return carry

    lax.fori_loop(0, i // 2, body, 0)

    @pl.when(i % 2 == 0)
    def _():
        accumulate(i, s0_ref, mb0_ref, True)

    @pl.when(i % 2 == 1)
    def _():
        page_phase(i // 2, 0)
        scores(i, s1_ref, mb1_ref)
        page_phase(i // 2, 1)
        accumulate(i - 1, s0_ref, mb0_ref, False)
        accumulate(i, s1_ref, mb1_ref, True)
        page_phase(i // 2, 2)
        page_phase(i // 2, 3)

    acc = acc_ref[...]
    o = acc[0:V_DIM] * (1.0 / acc[V_DIM:V_DIM + 1])
    lam = _lambda(lamv_ref[...], lam_init)
    d = o[:, :t] - lam * o[:, t:]
    ms = jnp.mean(d * d, axis=0, keepdims=True)
    out = d * lax.rsqrt(ms + EPS) * (gatt_ref[...] * (1.0 - lam_init))
    o_ref[0] = out.T.astype(BF16)


def _page_slots(bsz, seq, t):
    nq = seq // t
    return bsz * N_HEADS * ((nq * nq) // 4)


def _prompt_attention(qa, ka, vt, lamv, gatt_col, *, t, lam_init, paged_args=None, cp=None, layer=0):
    b, s, _ = qa.shape
    nq = s // t
    in_specs = [pl.BlockSpec((1, t, 2 * LANE), lambda bb, h, i, *_: (bb, i, h)),
                pl.BlockSpec((1, s, 2 * LANE), lambda bb, h, i, *_: (bb, 0, h)),
                pl.BlockSpec((1, nq, 1, VT_ROWS, t), lambda bb, h, i, *_: (bb, 0, h, 0, 0)),
                pl.BlockSpec((4, QK_DIM), lambda bb, h, i, *_: (0, 0)),
                pl.BlockSpec((V_DIM, 1), lambda bb, h, i, *_: (0, 0))]
    out_specs = [pl.BlockSpec((1, t, LANE), lambda bb, h, i, *_: (bb, i, h))]
    out_shape = [jax.ShapeDtypeStruct((b, s, ATT_W), BF16)]
    scratch = [pltpu.VMEM((1, 2 * t), F32),
               pltpu.VMEM((VT_ROWS, 2 * t), F32),
               pltpu.VMEM((t, 2 * t), F32),
               pltpu.VMEM((t, 2 * t), F32),
               pltpu.VMEM((1, 2 * t), F32),
               pltpu.VMEM((1, 2 * t), F32)]
    args = (qa, ka, vt, lamv, gatt_col)
    paged = None
    if paged_args is not None:
        page_table, wq, k_new, v_new = paged_args[:4]
        db, t_new, _ = k_new.shape
        nc = page_table.shape[1] // cp
        total = db * nc
        p_in, p_out, p_scratch = _paged_specs(wq, k_new, v_new, cp, lambda nd: (lambda bb, h, i, pt: (0,) * nd))
        in_specs += p_in
        out_specs.append(p_out)
        out_shape.append(jax.ShapeDtypeStruct((db, t_new, ATT_W), F32))
        scratch += p_scratch
        args = (page_table,) + args + tuple(paged_args[1:])
        paged = (("cp", cp), ("nc", nc), ("total", total), ("layer", layer), ("t_new", t_new),
                 ("per_slot", -(-total // _page_slots(b, s, t))), ("exact", total == _page_slots(b, s, t)),
                 ("nq", nq))
    outs = pl.pallas_call(
        functools.partial(_pattn_body, t=t, lam_init=lam_init, paged=paged),
        grid_spec=pltpu.PrefetchScalarGridSpec(
            num_scalar_prefetch=0 if paged is None else 1, grid=(b, N_HEADS, nq),
            in_specs=in_specs, out_specs=out_specs, scratch_shapes=scratch),
        out_shape=out_shape,
        compiler_params=_cparams(("arbitrary", "arbitrary", "arbitrary")),
        name="prompt_attention",
    )(*args)
    return (outs[0], outs[1]) if paged is not None else (outs[0], None)


def _paged_chunk_fns(pt_ref, wq_ref, knew_ref, vnew_ref, slope_ref, qpos_ref, lamv_ref, gatt_ref,
                     ck_hbm, cv_hbm, o_ref, kbuf, vbuf, sem, m_ref, l_ref, acc_ref, pad_ref, p_ref, al_ref,
                     *, cp, nc, total, layer, t_new, lam_init):
    keys = cp * PAGE
    hrows = 2 * t_new

    def copies(bb, cc, sl):
        out = []
        for p in range(cp):
            page = pt_ref[bb, cc * cp + p]
            out.append(pltpu.make_async_copy(ck_hbm.at[layer, page], kbuf.at[sl, :, pl.ds(p * PAGE, PAGE)],
                                             sem.at[0, sl]))
            out.append(pltpu.make_async_copy(cv_hbm.at[layer, page],
                                             vbuf.at[sl, pl.ds(p * PAGE * N_HEADS, PAGE * N_HEADS), :],
                                             sem.at[1, sl]))
        return out

    def prime():
        for cpy in copies(0, 0, 0):
            cpy.start()

    def probs(s):
        m_old = m_ref[...]
        m_new = jnp.maximum(m_old, jnp.max(s, axis=-1, keepdims=True))
        alpha = jnp.exp(m_old - m_new)
        p = jnp.exp(s - m_new)
        l_ref[...] = alpha * l_ref[...] + jnp.sum(p, axis=-1, keepdims=True)
        m_ref[...] = m_new
        return p, alpha

    def weigh(p, alpha, head_values):
        for hh in range(N_HEADS):
            rs = slice(hh * hrows, (hh + 1) * hrows)
            acc_ref[rs, :] = alpha[rs] * acc_ref[rs, :] + _dot(p[rs], head_values(hh))

    def top(g):
        slot = g % 2

        @pl.when(g + 1 < total)
        def _():
            gn = g + 1
            for cpy in copies(gn // nc, gn % nc, 1 - slot):
                cpy.start()

        @pl.when(g % nc == 0)
        def _():
            m_ref[...] = jnp.full(m_ref.shape, NEG, F32)
            l_ref[...] = jnp.zeros(l_ref.shape, F32)
            acc_ref[...] = jnp.zeros(acc_ref.shape, F32)

        for cpy in copies(g // nc, g % nc, slot):
            cpy.wait()

    def qk(g):
        c = g % nc
        kpos = (c * keys + lax.broadcasted_iota(jnp.int32, (1, keys), 1)).astype(F32)
        s = _dot(wq_ref[g // nc], kbuf[g % 2]) - slope_ref[...] * (qpos_ref[...] - kpos)
        p_ref[...], al_ref[...] = probs(s)

    def pv(g):
        vslot = vbuf.at[g % 2]
        weigh(p_ref[...], al_ref[...], lambda hh: vslot[pl.ds(hh, keys, stride=N_HEADS), :])

    def bottom(g):
        b = g // nc

        @pl.when(g % nc == nc - 1)
        def _():
            wq = wq_ref[b]
            slope = slope_ref[...]
            pad_ref[...] = jnp.zeros(pad_ref.shape, F32)
            pad_ref[0:t_new, :] = knew_ref[b]
            s2 = _dot_nt(wq, pad_ref[...])
            tk = lax.broadcasted_iota(jnp.int32, s2.shape, 1)
            tq = lax.broadcasted_iota(jnp.int32, s2.shape, 0) % t_new
            s2 = jnp.where(tk <= tq, s2 - slope * (tq - tk).astype(F32), NEG)
            pad_ref[0:t_new, :] = vnew_ref[b]
            p2, alpha2 = probs(s2)
            weigh(p2, alpha2, lambda hh: pad_ref[:, hh * V_DIM:(hh + 1) * V_DIM])

            o = acc_ref[...] * (1.0 / l_ref[...])
            lam = _lambda(lamv_ref[...], lam_init)
            gatt = gatt_ref[...] * (1.0 - lam_init)
            for hh in range(N_HEADS):
                r0 = hh * hrows
                d = o[r0:r0 + t_new] - lam * o[r0 + t_new:r0 + hrows]
                ms = jnp.mean(d * d, axis=-1, keepdims=True)
                o_ref[b, :, hh * V_DIM:(hh + 1) * V_DIM] = d * lax.rsqrt(ms + EPS) * gatt

    return prime, (top, qk, pv, bottom)


def _paged_specs(wq, k_new, v_new, cp, index_map):
    db, t_new, _ = k_new.shape
    rows_q = wq.shape[1]
    in_specs = [pl.BlockSpec(wq.shape, index_map(3)),
                pl.BlockSpec(k_new.shape, index_map(3)),
                pl.BlockSpec(v_new.shape, index_map(3)),
                pl.BlockSpec((rows_q, 1), index_map(2)),
                pl.BlockSpec((rows_q, 1), index_map(2)),
                pl.BlockSpec((4, QK_DIM), index_map(2)),
                pl.BlockSpec((1, V_DIM), index_map(2)),
                pl.BlockSpec(memory_space=pl.ANY),
                pl.BlockSpec(memory_space=pl.ANY)]
    out_spec = pl.BlockSpec((db, t_new, ATT_W), index_map(3))
    scratch = [pltpu.VMEM((2, Q_W, cp * PAGE), F32),
               pltpu.VMEM((2, cp * PAGE * N_HEADS, V_DIM), F32),
               pltpu.SemaphoreType.DMA((2, 2)),
               pltpu.VMEM((rows_q, 1), F32),
               pltpu.VMEM((rows_q, 1), F32),
               pltpu.VMEM((rows_q, V_DIM), F32),
               pltpu.VMEM((PAGE, Q_W), F32),
               pltpu.VMEM((rows_q, cp * PAGE), F32),
               pltpu.VMEM((rows_q, 1), F32)]
    return in_specs, out_spec, scratch


def _sattn_body(pt_ref, *refs, **static):
    prime, phases = _paged_chunk_fns(pt_ref, *refs, **static)
    g = pl.program_id(0)

    @pl.when(g == 0)
    def _():
        prime()

    for phase in phases:
        phase(g)


def _sample_attention(page_table, wq, k_new, v_new, slope_col, qpos_col, lamv, gatt_row,
                      cache_kt, cache_v4, *, cp, layer, lam_init):
    db, t_new, _ = k_new.shape
    nc = page_table.shape[1] // cp
    total = db * nc
    in_specs, out_spec, scratch = _paged_specs(wq, k_new, v_new, cp, lambda nd: (lambda g, pt: (0,) * nd))
    return pl.pallas_call(
        functools.partial(_sattn_body, cp=cp, nc=nc, total=total, layer=layer, t_new=t_new, lam_init=lam_init),
        grid_spec=pltpu.PrefetchScalarGridSpec(
            num_scalar_prefetch=1, grid=(total,), in_specs=in_specs, out_specs=out_spec, scratch_shapes=scratch),
        out_shape=jax.ShapeDtypeStruct((db, t_new, ATT_W), F32),
        compiler_params=_cparams(("arbitrary",)),
        name="sample_attention",
    )(page_table, wq, k_new, v_new, slope_col, qpos_col, lamv, gatt_row, cache_kt, cache_v4)


def _post_body(x_ref, att_ref, u_ref, halo_ref, gt1_ref, sh2_ref, sc2_ref, wdw_ref, bdw_ref, gln_ref, bln_ref,
               wpw_ref, wout_ref, gffn_ref, wr_ref, br_ref, x1_ref, h2_ref, lg_ref, useq_ref, aux_ref,
               *, tm, seq_rows):
    wdw = wdw_ref[...]
    off = HALO - (CONV_K - 1)

    def taps(rows, read):
        acc = jnp.zeros((rows, wdw.shape[1]), F32) + bdw_ref[...]
        for j in range(CONV_K):
            acc = acc + wdw[j:j + 1, :] * read(off + j)
        return acc

    if seq_rows is None:
        halo = halo_ref[0]
        halo = jnp.where(pl.program_id(1) > 0, halo, jnp.zeros_like(halo))
        useq_ref[0:HALO, :] = halo
        useq_ref[HALO:HALO + tm, :] = u_ref[0]
        rows = tm + HALO - 8
        for s in range(1, 8):
            aux_ref[s - 1, 0:rows, :] = useq_ref[s:s + rows, :]

        def read(o):
            s = o % 8
            src = useq_ref if s == 0 else aux_ref.at[s - 1]
            return src[o - s:o - s + tm, :]

        acc = taps(tm, read)
    else:
        def one_seq(b, carry):
            useq_ref[0:HALO, :] = halo_ref[b]
            useq_ref[HALO:HALO + seq_rows, :] = u_ref[b]
            aux_ref[pl.ds(pl.multiple_of(b * seq_rows, seq_rows), seq_rows), :] = taps(
                seq_rows, lambda o: useq_ref[o:o + seq_rows, :])
            return carry

        lax.fori_loop(0, tm // seq_rows, one_seq, 0)
        acc = aux_ref[...]
    mu = jnp.mean(acc, axis=-1, keepdims=True)
    cen = acc - mu
    var = jnp.mean(cen * cen, axis=-1, keepdims=True)
    y = cen * lax.rsqrt(var + EPS) * gln_ref[...] + bln_ref[...]
    y = y * _sigmoid(y)
    conv = _dot(y.astype(BF16), wpw_ref[...])
    att = att_ref[0].astype(BF16)
    merged = _dot(att, wout_ref[0:ATT_W, :]) + _dot(conv.astype(BF16), wout_ref[ATT_W:, :])
    x1 = x_ref[0] + gt1_ref[0] * merged
    x1_ref[0] = x1
    h2 = x1 * lax.rsqrt(jnp.mean(x1 * x1, axis=-1, keepdims=True) + EPS) * gffn_ref[...]
    h2 = h2 * (1.0 + sc2_ref[0]) + sh2_ref[0]
    hi, lo = _split_bf16(h2)
    h2_ref[0] = hi
    z = _dot(hi, wr_ref[...]) + _dot(lo, wr_ref[...])
    lg_ref[0] = z[:, :LANE] + z[:, LANE:] + br_ref[...]


def _post(x3, att3, u3, halo_src, gt1, sh2, sc2, wdw, bdw, gln, bln, wpw_b, wout_b, gffn, wr2, br,
          *, tm, seq_rows):
    nb, s, d = x3.shape
    nt = s // tm
    ch = u3.shape[-1]
    tile = lambda b, i: (b, i, 0)
    mod = lambda b, i: (b, 0, 0)
    const = lambda b, i: (0, 0)
    mod_rows = gt1.shape[1]
    if seq_rows is None:
        u_spec = pl.BlockSpec((1, tm, ch), tile)
        halo_spec = pl.BlockSpec((1, HALO, ch), lambda b, i: (b, jnp.maximum(i * (tm // HALO) - 1, 0), 0))
        scratch = [pltpu.VMEM((HALO + tm, ch), F32), pltpu.VMEM((7, HALO + tm, ch), F32)]
    else:
        u_spec = pl.BlockSpec(u3.shape, lambda b, i: (0, 0, 0))
        halo_spec = pl.BlockSpec(halo_src.shape, lambda b, i: (0, 0, 0))
        scratch = [pltpu.VMEM((HALO + seq_rows, ch), F32), pltpu.VMEM((tm, ch), F32)]
    return pl.pallas_call(
        functools.partial(_post_body, tm=tm, seq_rows=seq_rows),
        grid=(nb, nt),
        in_specs=[pl.BlockSpec((1, tm, d), tile),
                  pl.BlockSpec((1, tm, ATT_W), tile),
                  u_spec,
                  halo_spec,
                  pl.BlockSpec((1, mod_rows, d), mod),
                  pl.BlockSpec((1, mod_rows, d), mod),
                  pl.BlockSpec((1, mod_rows, d), mod),
                  pl.BlockSpec((CONV_K, ch), const),
                  pl.BlockSpec((1, ch), const),
                  pl.BlockSpec((1, ch), const),
                  pl.BlockSpec((1, ch), const),
                  pl.BlockSpec((ch, ch), const),
                  pl.BlockSpec((d, d), const),
                  pl.BlockSpec((1, d), const),
                  pl.BlockSpec((d, 2 * LANE), const),
                  pl.BlockSpec((1, LANE), const)],
        out_specs=[pl.BlockSpec((1, tm, d), tile),
                   pl.BlockSpec((1, tm, d), tile),
                   pl.BlockSpec((1, tm, LANE), tile)],
        out_shape=[jax.ShapeDtypeStruct((nb, s, d), F32),
                   jax.ShapeDtypeStruct((nb, s, d), BF16),
                   jax.ShapeDtypeStruct((nb, s, LANE), F32)],
        scratch_shapes=scratch,
        compiler_params=_cparams(("arbitrary", "arbitrary")),
        name="post_prompt" if seq_rows is None else "post_sample",
    )(x3, att3, u3, halo_src, gt1, sh2, sc2, wdw, bdw, gln, bln, wpw_b, wout_b, gffn, wr2, br)


def _combine_weights(lg):
    lane = lax.broadcasted_iota(jnp.int32, lg.shape, 1).astype(F32)
    big = 1e4
    gl = jnp.where(lane < N_GROUPS, lg, NEG)
    gmax = jnp.max(gl, axis=-1, keepdims=True)
    gsum = jnp.sum(jnp.exp(gl - gmax), axis=-1, keepdims=True)
    g_val = 1.0 / gsum
    g_idx = jnp.min(jnp.where(gl == gmax, lane, big), axis=-1, keepdims=True)
    lo = N_GROUPS + EXPERTS_PER_GROUP * g_idx
    el = jnp.where(lane >= lo, jnp.where(lane < lo + EXPERTS_PER_GROUP, lg, NEG), NEG)
    v1 = jnp.max(el, axis=-1, keepdims=True)
    i1 = jnp.min(jnp.where(el == v1, lane, big), axis=-1, keepdims=True)
    el2 = jnp.where(lane == i1, NEG, el)
    v2 = jnp.max(el2, axis=-1, keepdims=True)
    i2 = jnp.min(jnp.where(el2 == v2, lane, big), axis=-1, keepdims=True)
    e21 = jnp.exp(v2 - v1)
    w1 = g_val / (1.0 + e21)
    w2 = w1 * e21
    return jnp.where(lane == i1, w1, 0.0) + jnp.where(lane == i2, w2, 0.0), g_idx


def _dot_tn(a, b):
    return lax.dot_general(a, b, (((0,), (0,)), ((), ())), preferred_element_type=F32)


def _moe_body(x1_ref, h2_ref, lg_ref, gt2_ref, wg_ref, wu_ref, wd_ref, o_ref, xs_ref, cs_ref, ys_ref, off_ref, *, bm):
    tm = h2_ref.shape[1]
    comb, g_idx = _combine_weights(lg_ref[0])
    lane = lax.broadcasted_iota(jnp.int32, (tm, LANE), 1).astype(F32)
    onehot = jnp.where(lane == g_idx, 1.0, 0.0)
    r_i = lax.broadcasted_iota(jnp.int32, (tm, tm), 0)
    c_i = lax.broadcasted_iota(jnp.int32, (tm, tm), 1)
    earlier = _dot(jnp.where(c_i < r_i, 1.0, 0.0).astype(BF16), onehot.astype(BF16))
    totals = jnp.sum(onehot, axis=0, keepdims=True)
    lane1 = lane[0:1]
    starts = jnp.zeros((1, LANE), F32)
    run = jnp.zeros((1, 1), F32)
    off_ref[0] = 0
    for g in range(N_GROUPS):
        starts = starts + jnp.where(lane1 == g, run, 0.0)
        run = run + jnp.sum(jnp.where(lane1 == g, totals, 0.0), axis=-1, keepdims=True)
        off_ref[g + 1] = jnp.sum(run).astype(jnp.int32)
    pos = jnp.sum(onehot * (earlier + starts), axis=-1, keepdims=True)
    perm = jnp.where(pos == c_i.astype(F32), 1.0, 0.0).astype(BF16)
    xs_ref[...] = _dot_tn(perm, h2_ref[0]).astype(BF16)
    c_hi, c_lo = _split_bf16(comb)
    cs_ref[...] = _dot_tn(perm, c_hi) + _dot_tn(perm, c_lo)
    ys_ref[...] = jnp.zeros(ys_ref.shape, F32)

    def block(k, carry):
        r0 = pl.multiple_of(k * bm, bm)

        def group(g, carry2):
            @pl.when((off_ref[g] < r0 + bm) & (off_ref[g + 1] > r0))
            def _():
                xb = xs_ref[pl.ds(r0, bm), :]
                cb = cs_ref[pl.ds(r0, bm), :]
                lane_b = lax.broadcasted_iota(jnp.int32, cb.shape, 1)
                yb = jnp.zeros((bm, ys_ref.shape[1]), F32)
                for ee in range(EXPERTS_PER_GROUP):
                    e = g * EXPERTS_PER_GROUP + ee
                    gate = _dot(xb, wg_ref[e])
                    up = _dot(xb, wu_ref[e])
                    ce = jnp.sum(jnp.where(lane_b == N_GROUPS + e, cb, 0.0), axis=-1, keepdims=True)
                    yb = yb + _dot((gate * _sigmoid(gate) * up * ce).astype(BF16), wd_ref[e])
                ys_ref[pl.ds(r0, bm), :] = ys_ref[pl.ds(r0, bm), :] + yb
            return carry2

        return lax.fori_loop(0, N_GROUPS, group, carry)

    lax.fori_loop(0, tm // bm, block, 0)
    y = _dot(perm, ys_ref[...].astype(BF16))
    o_ref[0] = x1_ref[0] + gt2_ref[0] * y


def _moe(x1, h2, lg, gt2, wg_b, wu_b, wd_b, *, tm):
    nb, s, d = x1.shape
    nt = s // tm
    mod_rows = gt2.shape[1]
    tile = lambda b, i: (b, i, 0)
    const3 = lambda b, i: (0, 0, 0)
    single = pl.Buffered(1)
    bm = min(tm, LANE)
    return pl.pallas_call(
        functools.partial(_moe_body, bm=bm),
        grid=(nb, nt),
        in_specs=[pl.BlockSpec((1, tm, d), tile),
                  pl.BlockSpec((1, tm, d), tile),
                  pl.BlockSpec((1, tm, LANE), tile),
                  pl.BlockSpec((1, mod_rows, d), lambda b, i: (b, 0, 0)),
                  pl.BlockSpec(wg_b.shape, const3, pipeline_mode=single),
                  pl.BlockSpec(wu_b.shape, const3, pipeline_mode=single),
                  pl.BlockSpec(wd_b.shape, const3, pipeline_mode=single)],
        out_specs=pl.BlockSpec((1, tm, d), tile),
        out_shape=jax.ShapeDtypeStruct((nb, s, d), F32),
        scratch_shapes=[pltpu.VMEM((tm, d), BF16),
                        pltpu.VMEM((tm, LANE), F32),
                        pltpu.VMEM((tm, d), F32),
                        pltpu.SMEM((N_GROUPS + 1,), jnp.int32)],
        compiler_params=_cparams(("arbitrary", "arbitrary")),
        name="moe",
    )(x1, h2, lg, gt2, wg_b, wu_b, wd_b)


def kernel(x_prompt, x_sample, c_prompt, c_sample, cache_k, cache_v, state_conv, page_table, w_ada, b_ada, g_norm_mix, w_in, g_q, g_k, lambda_q1, lambda_k1, lambda_q2, lambda_k2, g_attn_out, w_dw, b_dw, g_conv_ln, b_conv_ln, w_conv_pw, w_out, g_norm_ffn, w_router_group, b_router_group, w_router_expert, b_router_expert, w_exp_gate, w_exp_up, w_exp_down):
    depth = w_in.shape[0]
    bsz, seq, d = x_prompt.shape
    db, t_new, _ = x_sample.shape
    n_pages = page_table.shape[1]
    past = n_pages * PAGE
    n_phys = cache_k.shape[1]
    conv_ch = w_dw.shape[-1]
    t_att = min(512, seq)
    tm_post = min(256, seq)
    tm_moe = min(512, seq)
    n_s = db * t_new

    blk = jnp.arange(MXU) // QK_DIM
    pm = jnp.where(blk[:, None] == blk[None, :], 1.0 / QK_DIM, 0.0).astype(BF16)
    rows_q = 2 * N_HEADS * t_new
    row_head = jnp.arange(rows_q) // (2 * t_new)
    slope_col = jnp.asarray(ALIBI_SLOPES, F32)[row_head].reshape(rows_q, 1)
    qpos_col = (past + jnp.arange(rows_q) % t_new).astype(F32).reshape(rows_q, 1)
    cache_kt = jnp.transpose(cache_k, (0, 1, 3, 4, 5, 2)).reshape(depth, n_phys, Q_W, PAGE)
    cache_v4 = cache_v.reshape(depth, n_phys, PAGE * N_HEADS, V_DIM)

    yp = x_prompt
    ys = x_sample
    kp_l, vp_l, cp_l, ks_l, vs_l, cs_l = [], [], [], [], [], []
    for l in range(depth):
        lam_init = 0.8 - 0.6 * math.exp(-0.3 * l)
        lamv = jnp.stack([lambda_q1[l], lambda_k1[l], lambda_q2[l], lambda_k2[l]]).astype(F32)
        w_in_b = w_in[l].astype(BF16)
        wpw_b = w_conv_pw[l].astype(BF16)
        wout_b = w_out[l].astype(BF16)
        wg_b = w_exp_gate[l].astype(BF16)
        wu_b = w_exp_up[l].astype(BF16)
        wd_b = w_exp_down[l].astype(BF16)
        gq_t = jnp.tile(g_q[l], 2 * N_HEADS).reshape(1, Q_W)
        gk_t = jnp.tile(g_k[l], 2 * N_HEADS).reshape(1, Q_W)
        gmix = g_norm_mix[l].reshape(1, d)
        gffn = g_norm_ffn[l].reshape(1, d)
        wr = jnp.zeros((d, LANE), F32)
        wr = wr.at[:, :N_GROUPS].set(w_router_group[l]).at[:, N_GROUPS:N_GROUPS + N_EXPERTS].set(w_router_expert[l])
        wr_hi = wr.astype(BF16)
        wr2 = jnp.concatenate([wr_hi, (wr - wr_hi.astype(F32)).astype(BF16)], axis=1)
        br = jnp.zeros((1, LANE), F32)
        br = br.at[0, :N_GROUPS].set(b_router_group[l]).at[0, N_GROUPS:N_GROUPS + N_EXPERTS].set(b_router_expert[l])
        conv_w = (w_dw[l], b_dw[l].reshape(1, conv_ch), g_conv_ln[l].reshape(1, conv_ch),
                  b_conv_ln[l].reshape(1, conv_ch), wpw_b, wout_b, gffn, wr2, br)

        mod = _adaln(jnp.concatenate([c_prompt, c_sample], axis=0), w_ada[l], b_ada[l])
        mod = mod.reshape(bsz + db, 6, 1, d)
        modp = [mod[:bsz, j] for j in range(6)]
        mods = [mod[bsz:, j] for j in range(6)]

        kt, v4, up, qa, ka, vt = _proj(
            yp.reshape(bsz * seq, d), modp[0], modp[1], gmix, w_in_b, gq_t, gk_t, pm,
            tm=t_att, tiles_per_seq=seq // t_att, prompt=True)
        rep = lambda m: jnp.broadcast_to(m, (db, t_new, d)).reshape(1, n_s, d)
        qsm, ksm, vsm, usm = _proj(
            ys.reshape(n_s, d), rep(mods[0]), rep(mods[1]), gmix, w_in_b, gq_t, gk_t, pm,
            tm=n_s, tiles_per_seq=1, prompt=False)
        q5 = (qsm * ATT_SCALE).reshape(db, t_new, 2 * N_HEADS, QK_DIM)
        eye = jnp.eye(2 * N_HEADS, dtype=F32)
        wq = (q5.transpose(0, 2, 1, 3)[:, :, :, None, :] * eye[None, :, None, :, None])
        wq = wq.reshape(db, rows_q, Q_W)
        paged_args = (page_table, wq, ksm.reshape(db, t_new, Q_W), vsm.reshape(db, t_new, ATT_W), slope_col,
                      qpos_col, lamv, g_attn_out[l].reshape(1, V_DIM), cache_kt, cache_v4)

        attn_args = (qa.reshape(bsz, seq, 2 * Q_W), ka.reshape(bsz, seq, 2 * Q_W), vt, lamv,
                     g_attn_out[l].reshape(V_DIM, 1))
        cp_fused = math.gcd(n_pages, 8)
        if db * (n_pages // cp_fused) <= 2 * _page_slots(bsz, seq, t_att):
            att_p, att_s = _prompt_attention(*attn_args, t=t_att, lam_init=lam_init,
                                             paged_args=paged_args, cp=cp_fused, layer=l)
        else:
            att_p, _ = _prompt_attention(*attn_args, t=t_att, lam_init=lam_init)
            att_s = _sample_attention(*paged_args, cp=math.gcd(n_pages, 16), layer=l, lam_init=lam_init)

        up3 = up.reshape(bsz, seq, conv_ch)
        x1p, h2p, lgp = _post(yp, att_p, up3, up3, modp[2], modp[3], modp[4], *conv_w,
                              tm=tm_post, seq_rows=None)
        yp = _moe(x1p, h2p, lgp, modp[5], wg_b, wu_b, wd_b, tm=tm_moe)

        us3 = usm.reshape(db, t_new, conv_ch)
        state_pad = jnp.pad(state_conv[l], ((0, 0), (HALO - (CONV_K - 1), 0), (0, 0)))
        x1s, h2s, lgs = _post(ys.reshape(1, n_s, d), att_s.reshape(1, n_s, ATT_W), us3, state_pad,
                              rep(mods[2]), rep(mods[3]), rep(mods[4]), *conv_w, tm=n_s, seq_rows=t_new)
        ys = _moe(x1s, h2s, lgs, rep(mods[5]), wg_b, wu_b, wd_b, tm=n_s).reshape(db, t_new, d)

        kp_l.append(jnp.transpose(kt.reshape(bsz, N_HEADS, 2, QK_DIM, seq), (0, 4, 1, 2, 3)))
        vp_l.append(v4.reshape(bsz, seq, N_HEADS, V_DIM))
        cp_l.append(up3[:, seq - (CONV_K - 1):])
        ks_l.append(ksm.reshape(db, t_new, N_HEADS, 2, QK_DIM))
        vs_l.append(vsm.reshape(db, t_new, N_HEADS, V_DIM))
        cs_l.append(jnp.concatenate([state_conv[l], us3], axis=1)[:, t_new:])

    return (yp, ys, jnp.stack(kp_l), jnp.stack(vp_l), jnp.stack(cp_l),
            jnp.stack(ks_l), jnp.stack(vs_l), jnp.stack(cs_l))
```

```python
import functools
import math

import jax
import jax.numpy as jnp
from jax import lax
from jax.experimental import pallas as pl
from jax.experimental.pallas import tpu as pltpu

F32 = jnp.float32
BF16 = jnp.bfloat16

N_HEADS = 4
QK_DIM = 64
V_DIM = 128
ATT_W = N_HEADS * V_DIM
Q_W = N_HEADS * 2 * QK_DIM
CONV_K = 31
N_GROUPS = 4
EXPERTS_PER_GROUP = 4
N_EXPERTS = N_GROUPS * EXPERTS_PER_GROUP
EPS = 1e-6
ATT_SCALE = 1.0 / math.sqrt(QK_DIM)
PAGE = 128
NEG = -1e30
LOG2E = math.log2(math.e)
ALIBI_SLOPES = tuple(2.0 ** (-8.0 * (i + 1) / N_HEADS) for i in range(N_HEADS))

LANE = 128
MXU = 256
HALO = 32
VT_ROWS = V_DIM + 16
VMEM_LIMIT = 56 * 1024 * 1024


def _cparams(sem):
    return pltpu.CompilerParams(dimension_semantics=sem, vmem_limit_bytes=VMEM_LIMIT)


def _dot(a, b):
    return jnp.dot(a, b, preferred_element_type=F32)


def _dot_nt(a, b):
    return lax.dot_general(a, b, (((1,), (1,)), ((), ())), preferred_element_type=F32)


def _split_bf16(x):
    hi = x.astype(BF16)
    lo = (x - hi.astype(F32)).astype(BF16)
    return hi, lo


def _sigmoid(x):
    return 1.0 / (1.0 + jnp.exp(-x))


def _adaln_body(c_ref, w_ref, b_ref, o_ref):
    c = c_ref[...]
    o_ref[...] = jnp.dot(c * _sigmoid(c), w_ref[...], preferred_element_type=F32,
                         precision=lax.Precision.HIGHEST) + b_ref[...]


def _adaln(c, w_ada, b_ada):
    nb, d = c.shape
    n_out = w_ada.shape[1]
    bn = 1024
    return pl.pallas_call(
        _adaln_body,
        grid=(n_out // bn,),
        in_specs=[pl.BlockSpec((nb, d), lambda j: (0, 0)),
                  pl.BlockSpec((d, bn), lambda j: (0, j)),
                  pl.BlockSpec((1, bn), lambda j: (0, j))],
        out_specs=pl.BlockSpec((nb, bn), lambda j: (0, j)),
        out_shape=jax.ShapeDtypeStruct((nb, n_out), F32),
        compiler_params=_cparams(("arbitrary",)),
        name="adaln",
    )(c, w_ada, b_ada.reshape(1, n_out))


def _proj_body(x_ref, sh_ref, sc_ref, gmix_ref, w_ref, gq_ref, gk_ref, pm_ref, *out_refs, prompt, tiles_per_seq):
    x = x_ref[...]
    tm = x.shape[0]
    h = x * lax.rsqrt(jnp.mean(x * x, axis=-1, keepdims=True) + EPS) * gmix_ref[...]
    h = h * (1.0 + sc_ref[0]) + sh_ref[0]
    hb = h.astype(BF16)
    pm = pm_ref[...]

    def chunk_rms(z, g):
        outs = []
        for c in range(Q_W // MXU):
            zc = z[:, c * MXU:(c + 1) * MXU]
            hi, lo = _split_bf16(zc * zc)
            ms = _dot(hi, pm) + _dot(lo, pm)
            outs.append(zc * lax.rsqrt(ms + EPS))
        return jnp.concatenate(outs, axis=-1) * g

    qn = chunk_rms(_dot(hb, w_ref[:, 0:Q_W]), gq_ref[...])
    kn = chunk_rms(_dot(hb, w_ref[:, Q_W:2 * Q_W]), gk_ref[...])
    v = _dot(hb, w_ref[:, 2 * Q_W:2 * Q_W + ATT_W])
    a = _dot(hb, w_ref[:, 2 * Q_W + ATT_W:])
    cw = a.shape[-1] // 2
    u = a[:, :cw] * _sigmoid(a[:, cw:])
    if not prompt:
        q_ref, k_ref, v_ref, u_ref = out_refs
        q_ref[...] = qn
        k_ref[...] = kn
        v_ref[...] = v
        u_ref[...] = u
        return

    kt_ref, v4_ref, u_ref, qa_ref, ka_ref, vt_ref = out_refs
    u_ref[...] = u
    kt_ref[0] = kn.T
    for hh in range(N_HEADS):
        v4_ref[pl.ds(hh, tm, stride=N_HEADS), :] = v[:, hh * V_DIM:(hh + 1) * V_DIM]
    vt = v.T
    for hh in range(N_HEADS):
        vt_ref[0, 0, hh, 0:V_DIM, :] = vt[hh * V_DIM:(hh + 1) * V_DIM, :].astype(BF16)
        vt_ref[0, 0, hh, V_DIM:VT_ROWS, :] = jnp.ones((VT_ROWS - V_DIM, tm), BF16)
    lane = lax.broadcasted_iota(jnp.int32, (tm, LANE), 1)
    pos = ((pl.program_id(0) % tiles_per_seq) * tm + lax.broadcasted_iota(jnp.int32, (tm, 1), 0)).astype(F32)
    q_aux = jnp.where(lane < QK_DIM + 3, 1.0, 0.0)
    qs = qn * (ATT_SCALE * LOG2E)
    for hh in range(N_HEADS):
        b = pos * (ALIBI_SLOPES[hh] * LOG2E)
        b_hi = b.astype(BF16).astype(F32)
        b_mid = (b - b_hi).astype(BF16).astype(F32)
        b_lo = b - b_hi - b_mid
        k_aux = jnp.where(lane == QK_DIM, b_hi,
                          jnp.where(lane == QK_DIM + 1, b_mid, jnp.where(lane == QK_DIM + 2, b_lo, 0.0)))
        qg = qs[:, hh * LANE:(hh + 1) * LANE]
        kg = kn[:, hh * LANE:(hh + 1) * LANE]
        for mp in range(2):
            if mp == 1:
                qg = pltpu.roll(qg, QK_DIM, 1)
                kg = pltpu.roll(kg, QK_DIM, 1)
            c0 = (2 * hh + mp) * LANE
            qa_ref[:, c0:c0 + LANE] = jnp.where(lane < QK_DIM, qg, q_aux).astype(BF16)
            ka_ref[:, c0:c0 + LANE] = jnp.where(lane < QK_DIM, kg, k_aux).astype(BF16)


def _proj(x2, sh, sc, gmix, w_in_b, gq_t, gk_t, pm, *, tm, tiles_per_seq, prompt):
    n, d = x2.shape
    nt = n // tm
    in_w = w_in_b.shape[1]
    conv_ch = (in_w - 2 * Q_W - ATT_W) // 2
    mod_rows = sh.shape[1]
    row = lambda i: (i, 0)
    const = lambda i: (0, 0)
    mod_idx = lambda i: (i // tiles_per_seq, 0, 0)
    in_specs = [pl.BlockSpec((tm, d), row),
                pl.BlockSpec((1, mod_rows, d), mod_idx),
                pl.BlockSpec((1, mod_rows, d), mod_idx),
                pl.BlockSpec((1, d), const),
                pl.BlockSpec((d, in_w), const),
                pl.BlockSpec((1, Q_W), const),
                pl.BlockSpec((1, Q_W), const),
                pl.BlockSpec((MXU, MXU), const)]
    if prompt:
        nb = nt // tiles_per_seq
        out_shape = [jax.ShapeDtypeStruct((nb, Q_W, tiles_per_seq * tm), F32),
                     jax.ShapeDtypeStruct((n * N_HEADS, V_DIM), F32),
                     jax.ShapeDtypeStruct((n, conv_ch), F32),
                     jax.ShapeDtypeStruct((n, 2 * Q_W), BF16),
                     jax.ShapeDtypeStruct((n, 2 * Q_W), BF16),
                     jax.ShapeDtypeStruct((nb, tiles_per_seq, N_HEADS, VT_ROWS, tm), BF16)]
        out_specs = [pl.BlockSpec((1, Q_W, tm), lambda i: (i // tiles_per_seq, 0, i % tiles_per_seq)),
                     pl.BlockSpec((tm * N_HEADS, V_DIM), row),
                     pl.BlockSpec((tm, conv_ch), row),
                     pl.BlockSpec((tm, 2 * Q_W), row),
                     pl.BlockSpec((tm, 2 * Q_W), row),
                     pl.BlockSpec((1, 1, N_HEADS, VT_ROWS, tm),
                                  lambda i: (i // tiles_per_seq, i % tiles_per_seq, 0, 0, 0))]
    else:
        f32_out = jax.ShapeDtypeStruct((n, Q_W), F32)
        f32_spec = pl.BlockSpec((tm, Q_W), row)
        out_shape = [f32_out, f32_out, f32_out, jax.ShapeDtypeStruct((n, conv_ch), F32)]
        out_specs = [f32_spec, f32_spec, f32_spec, pl.BlockSpec((tm, conv_ch), row)]
    return pl.pallas_call(
        functools.partial(_proj_body, prompt=prompt, tiles_per_seq=tiles_per_seq),
        grid=(nt,),
        in_specs=in_specs,
        out_specs=out_specs,
        out_shape=out_shape,
        compiler_params=_cparams(("arbitrary",)),
        name="proj_prompt" if prompt else "proj_sample",
    )(x2, sh, sc, gmix, w_in_b, gq_t, gk_t, pm)


def _lambda(lamv, lam_init):
    a = jnp.sum(lamv[0:1] * lamv[1:2], axis=-1, keepdims=True)
    b = jnp.sum(lamv[2:3] * lamv[3:4], axis=-1, keepdims=True)
    return jnp.exp(a) - jnp.exp(b) + lam_init


def _pattn_body(*refs, t, lam_init, paged):
    if paged is None:
        (q_ref, k_ref, vt_ref, lamv_ref, gatt_ref, o_ref,
         m_ref, acc_ref, s0_ref, s1_ref, mb0_ref, mb1_ref) = refs

        def page_phase(idx, which):
            pass
    else:
        pt_ref, q_ref, k_ref, vt_ref, lamv_ref, gatt_ref = refs[:6]
        o_ref, os_ref, m_ref, acc_ref, s0_ref, s1_ref, mb0_ref, mb1_ref = refs[15:23]
        static = dict(paged)
        per_slot = static.pop("per_slot")
        paged_exact = static.pop("exact")
        nq = static.pop("nq")
        prime, phases = _paged_chunk_fns(pt_ref, *refs[6:15], os_ref, *refs[23:], lam_init=lam_init, **static)
        first = (pl.program_id(0) == 0) & (pl.program_id(1) == 0) & (pl.program_id(2) == 0)

        @pl.when(first)
        def _():
            prime()

        qi = pl.program_id(2)
        slots_bh = (nq * nq) // 4
        slot0 = (pl.program_id(0) * N_HEADS + pl.program_id(1)) * slots_bh + (qi * qi) // 4

        def page_phase(idx, which):
            if per_slot == 1 and paged_exact:
                phases[which](slot0 + idx)
            elif which == len(phases) - 1:
                for r in range(per_slot):
                    g = (slot0 + idx) * per_slot + r

                    @pl.when(g < static["total"])
                    def _():
                        for phase in phases:
                            phase(g)

    i = pl.program_id(2)
    m_ref[...] = jnp.full(m_ref.shape, NEG, F32)
    acc_ref[...] = jnp.zeros(acc_ref.shape, F32)

    def scores(j, s_ref, mb_ref):
        kblk = k_ref[0, pl.ds(pl.multiple_of(j * t, t), t), :]
        for mp in range(2):
            sl = slice(mp * t, (mp + 1) * t)
            s = _dot_nt(kblk[:, mp * LANE:(mp + 1) * LANE], q_ref[0, :, mp * LANE:(mp + 1) * LANE])
            s_ref[:, sl] = s
            mb_ref[:, sl] = jnp.max(s, axis=0, keepdims=True)

    def accumulate(j, s_ref, mb_ref, masked):
        vt = vt_ref[0, j, 0]
        for mp in range(2):
            sl = slice(mp * t, (mp + 1) * t)
            s = s_ref[:, sl]
            if masked:
                r = lax.broadcasted_iota(jnp.int32, s.shape, 0)
                c = lax.broadcasted_iota(jnp.int32, s.shape, 1)
                s = jnp.where(r <= c, s, NEG)
                mb = jnp.max(s, axis=0, keepdims=True)
            else:
                mb = mb_ref[:, sl]
            m_old = m_ref[:, sl]
            m_new = jnp.maximum(m_old, mb)
            alpha = jnp.exp2(m_old - m_new)
            p = jnp.exp2(s - m_new).astype(BF16)
            acc_ref[:, sl] = alpha * acc_ref[:, sl] + _dot(vt, p)
            m_ref[:, sl] = m_new

    scores(0, s0_ref, mb0_ref)

    def body(jj, carry):
        j = 2 * jj
        page_phase(jj, 0)
        scores(j + 1, s1_ref, mb1_ref)
        page_phase(jj, 1)
        accumulate(j, s0_ref, mb0_ref, False)
        scores(j + 2, s0_ref, mb0_ref)
        accumulate(j + 1, s1_ref, mb1_ref, False)
        page_phase(jj, 2)
        page_phase(jj, 3)
        return carry

    lax.fori_loop(0, i // 2, body, 0)

    @pl.when(i % 2 == 0)
    def _():
        accumulate(i, s0_ref, mb0_ref, True)

    @pl.when(i % 2 == 1)
    def _():
        page_phase(i // 2, 0)
        scores(i, s1_ref, mb1_ref)
        page_phase(i // 2, 1)
        accumulate(i - 1, s0_ref, mb0_ref, False)
        accumulate(i, s1_ref, mb1_ref, True)
        page_phase(i // 2, 2)
        page_phase(i // 2, 3)

    acc = acc_ref[...]
    o = acc[0:V_DIM] * (1.0 / acc[V_DIM:V_DIM + 1])
    lam = _lambda(lamv_ref[...], lam_init)
    d = o[:, :t] - lam * o[:, t:]
    ms = jnp.mean(d * d, axis=0, keepdims=True)
    out = d * lax.rsqrt(ms + EPS) * (gatt_ref[...] * (1.0 - lam_init))
    o_ref[0] = out.T.astype(BF16)


def _page_slots(bsz, seq, t):
    nq = seq // t
    return bsz * N_HEADS * ((nq * nq) // 4)


def _prompt_attention(qa, ka, vt, lamv, gatt_col, *, t, lam_init, paged_args=None, cp=None, layer=0):
    b, s, _ = qa.shape
    nq = s // t
    in_specs = [pl.BlockSpec((1, t, 2 * LANE), lambda bb, h, i, *_: (bb, i, h)),
                pl.BlockSpec((1, s, 2 * LANE), lambda bb, h, i, *_: (bb, 0, h)),
                pl.BlockSpec((1, nq, 1, VT_ROWS, t), lambda bb, h, i, *_: (bb, 0, h, 0, 0)),
                pl.BlockSpec((4, QK_DIM), lambda bb, h, i, *_: (0, 0)),
                pl.BlockSpec((V_DIM, 1), lambda bb, h, i, *_: (0, 0))]
    out_specs = [pl.BlockSpec((1, t, LANE), lambda bb, h, i, *_: (bb, i, h))]
    out_shape = [jax.ShapeDtypeStruct((b, s, ATT_W), BF16)]
    scratch = [pltpu.VMEM((1, 2 * t), F32),
               pltpu.VMEM((VT_ROWS, 2 * t), F32),
               pltpu.VMEM((t, 2 * t), F32),
               pltpu.VMEM((t, 2 * t), F32),
               pltpu.VMEM((1, 2 * t), F32),
               pltpu.VMEM((1, 2 * t), F32)]
    args = (qa, ka, vt, lamv, gatt_col)
    paged = None
    if paged_args is not None:
        page_table, wq, k_new, v_new = paged_args[:4]
        db, t_new, _ = k_new.shape
        nc = page_table.shape[1] // cp
        total = db * nc
        p_in, p_out, p_scratch = _paged_specs(wq, k_new, v_new, cp, lambda nd: (lambda bb, h, i, pt: (0,) * nd))
        in_specs += p_in
        out_specs.append(p_out)
        out_shape.append(jax.ShapeDtypeStruct((db, t_new, ATT_W), F32))
        scratch += p_scratch
        args = (page_table,) + args + tuple(paged_args[1:])
        paged = (("cp", cp), ("nc", nc), ("total", total), ("layer", layer), ("t_new", t_new),
                 ("per_slot", -(-total // _page_slots(b, s, t))), ("exact", total == _page_slots(b, s, t)),
                 ("nq", nq))
    outs = pl.pallas_call(
        functools.partial(_pattn_body, t=t, lam_init=lam_init, paged=paged),
        grid_spec=pltpu.PrefetchScalarGridSpec(
            num_scalar_prefetch=0 if paged is None else 1, grid=(b, N_HEADS, nq),
            in_specs=in_specs, out_specs=out_specs, scratch_shapes=scratch),
        out_shape=out_shape,
        compiler_params=_cparams(("arbitrary", "arbitrary", "arbitrary")),
        name="prompt_attention",
    )(*args)
    return (outs[0], outs[1]) if paged is not None else (outs[0], None)


def _paged_chunk_fns(pt_ref, wq_ref, knew_ref, vnew_ref, slope_ref, qpos_ref, lamv_ref, gatt_ref,
                     ck_hbm, cv_hbm, o_ref, kbuf, vbuf, sem, m_ref, l_ref, acc_ref, pad_ref, p_ref, al_ref,
                     *, cp, nc, total, layer, t_new, lam_init):
    keys = cp * PAGE
    hrows = 2 * t_new

    def copies(bb, cc, sl):
        out = []
        for p in range(cp):
            page = pt_ref[bb, cc * cp + p]
            out.append(pltpu.make_async_copy(ck_hbm.at[layer, page], kbuf.at[sl, :, pl.ds(p * PAGE, PAGE)],
                                             sem.at[0, sl]))
            out.append(pltpu.make_async_copy(cv_hbm.at[layer, page],
                                             vbuf.at[sl, pl.ds(p * PAGE * N_HEADS, PAGE * N_HEADS), :],
                                             sem.at[1, sl]))
        return out

    def prime():
        for cpy in copies(0, 0, 0):
            cpy.start()

    def probs(s):
        m_old = m_ref[...]
        m_new = jnp.maximum(m_old, jnp.max(s, axis=-1, keepdims=True))
        alpha = jnp.exp(m_old - m_new)
        p = jnp.exp(s - m_new)
        l_ref[...] = alpha * l_ref[...] + jnp.sum(p, axis=-1, keepdims=True)
        m_ref[...] = m_new
        return p, alpha

    def weigh(p, alpha, head_values):
        for hh in range(N_HEADS):
            rs = slice(hh * hrows, (hh + 1) * hrows)
            acc_ref[rs, :] = alpha[rs] * acc_ref[rs, :] + _dot(p[rs], head_values(hh))

    def top(g):
        slot = g % 2

        @pl.when(g + 1 < total)
        def _():
            gn = g + 1
            for cpy in copies(gn // nc, gn % nc, 1 - slot):
                cpy.start()

        @pl.when(g % nc == 0)
        def _():
            m_ref[...] = jnp.full(m_ref.shape, NEG, F32)
            l_ref[...] = jnp.zeros(l_ref.shape, F32)
            acc_ref[...] = jnp.zeros(acc_ref.shape, F32)

        for cpy in copies(g // nc, g % nc, slot):
            cpy.wait()

    def qk(g):
        c = g % nc
        kpos = (c * keys + lax.broadcasted_iota(jnp.int32, (1, keys), 1)).astype(F32)
        s = _dot(wq_ref[g // nc], kbuf[g % 2]) - slope_ref[...] * (qpos_ref[...] - kpos)
        p_ref[...], al_ref[...] = probs(s)

    def pv(g):
        vslot = vbuf.at[g % 2]
        weigh(p_ref[...], al_ref[...], lambda hh: vslot[pl.ds(hh, keys, stride=N_HEADS), :])

    def bottom(g):
        b = g // nc

        @pl.when(g % nc == nc - 1)
        def _():
            wq = wq_ref[b]
            slope = slope_ref[...]
            pad_ref[...] = jnp.zeros(pad_ref.shape, F32)
            pad_ref[0:t_new, :] = knew_ref[b]
            s2 = _dot_nt(wq, pad_ref[...])
            tk = lax.broadcasted_iota(jnp.int32, s2.shape, 1)
            tq = lax.broadcasted_iota(jnp.int32, s2.shape, 0) % t_new
            s2 = jnp.where(tk <= tq, s2 - slope * (tq - tk).astype(F32), NEG)
            pad_ref[0:t_new, :] = vnew_ref[b]
            p2, alpha2 = probs(s2)
            weigh(p2, alpha2, lambda hh: pad_ref[:, hh * V_DIM:(hh + 1) * V_DIM])

            o = acc_ref[...] * (1.0 / l_ref[...])
            lam = _lambda(lamv_ref[...], lam_init)
            gatt = gatt_ref[...] * (1.0 - lam_init)
            for hh in range(N_HEADS):
                r0 = hh * hrows
                d = o[r0:r0 + t_new] - lam * o[r0 + t_new:r0 + hrows]
                ms = jnp.mean(d * d, axis=-1, keepdims=True)
                o_ref[b, :, hh * V_DIM:(hh + 1) * V_DIM] = d * lax.rsqrt(ms + EPS) * gatt

    return prime, (top, qk, pv, bottom)


def _paged_specs(wq, k_new, v_new, cp, index_map):
    db, t_new, _ = k_new.shape
    rows_q = wq.shape[1]
    in_specs = [pl.BlockSpec(wq.shape, index_map(3)),
                pl.BlockSpec(k_new.shape, index_map(3)),
                pl.BlockSpec(v_new.shape, index_map(3)),
                pl.BlockSpec((rows_q, 1), index_map(2)),
                pl.BlockSpec((rows_q, 1), index_map(2)),
                pl.BlockSpec((4, QK_DIM), index_map(2)),
                pl.BlockSpec((1, V_DIM), index_map(2)),
                pl.BlockSpec(memory_space=pl.ANY),
                pl.BlockSpec(memory_space=pl.ANY)]
    out_spec = pl.BlockSpec((db, t_new, ATT_W), index_map(3))
    scratch = [pltpu.VMEM((2, Q_W, cp * PAGE), F32),
               pltpu.VMEM((2, cp * PAGE * N_HEADS, V_DIM), F32),
               pltpu.SemaphoreType.DMA((2, 2)),
               pltpu.VMEM((rows_q, 1), F32),
               pltpu.VMEM((rows_q, 1), F32),
               pltpu.VMEM((rows_q, V_DIM), F32),
               pltpu.VMEM((PAGE, Q_W), F32),
               pltpu.VMEM((rows_q, cp * PAGE), F32),
               pltpu.VMEM((rows_q, 1), F32)]
    return in_specs, out_spec, scratch


def _sattn_body(pt_ref, *refs, **static):
    prime, phases = _paged_chunk_fns(pt_ref, *refs, **static)
    g = pl.program_id(0)

    @pl.when(g == 0)
    def _():
        prime()

    for phase in phases:
        phase(g)


def _sample_attention(page_table, wq, k_new, v_new, slope_col, qpos_col, lamv, gatt_row,
                      cache_kt, cache_v4, *, cp, layer, lam_init):
    db, t_new, _ = k_new.shape
    nc = page_table.shape[1] // cp
    total = db * nc
    in_specs, out_spec, scratch = _paged_specs(wq, k_new, v_new, cp, lambda nd: (lambda g, pt: (0,) * nd))
    return pl.pallas_call(
        functools.partial(_sattn_body, cp=cp, nc=nc, total=total, layer=layer, t_new=t_new, lam_init=lam_init),
        grid_spec=pltpu.PrefetchScalarGridSpec(
            num_scalar_prefetch=1, grid=(total,), in_specs=in_specs, out_specs=out_spec, scratch_shapes=scratch),
        out_shape=jax.ShapeDtypeStruct((db, t_new, ATT_W), F32),
        compiler_params=_cparams(("arbitrary",)),
        name="sample_attention",
    )(page_table, wq, k_new, v_new, slope_col, qpos_col, lamv, gatt_row, cache_kt, cache_v4)


def _post_body(x_ref, att_ref, u_ref, halo_ref, gt1_ref, sh2_ref, sc2_ref, wdw_ref, bdw_ref, gln_ref, bln_ref,
               wpw_ref, wout_ref, gffn_ref, wr_ref, br_ref, x1_ref, h2_ref, lg_ref, useq_ref, aux_ref,
               *, tm, seq_rows):
    wdw = wdw_ref[...]
    off = HALO - (CONV_K - 1)

    def taps(rows, read):
        acc = jnp.zeros((rows, wdw.shape[1]), F32) + bdw_ref[...]
        for j in range(CONV_K):
            acc = acc + wdw[j:j + 1, :] * read(off + j)
        return acc

    if seq_rows is None:
        halo = halo_ref[0]
        halo = jnp.where(pl.program_id(1) > 0, halo, jnp.zeros_like(halo))
        useq_ref[0:HALO, :] = halo
        useq_ref[HALO:HALO + tm, :] = u_ref[0]
        rows = tm + HALO - 8
        for s in range(1, 8):
            aux_ref[s - 1, 0:rows, :] = useq_ref[s:s + rows, :]

        def read(o):
            s = o % 8
            src = useq_ref if s == 0 else aux_ref.at[s - 1]
            return src[o - s:o - s + tm, :]

        acc = taps(tm, read)
    else:
        def one_seq(b, carry):
            useq_ref[0:HALO, :] = halo_ref[b]
            useq_ref[HALO:HALO + seq_rows, :] = u_ref[b]
            aux_ref[pl.ds(pl.multiple_of(b * seq_rows, seq_rows), seq_rows), :] = taps(
                seq_rows, lambda o: useq_ref[o:o + seq_rows, :])
            return carry

        lax.fori_loop(0, tm // seq_rows, one_seq, 0)
        acc = aux_ref[...]
    mu = jnp.mean(acc, axis=-1, keepdims=True)
    cen = acc - mu
    var = jnp.mean(cen * cen, axis=-1, keepdims=True)
    y = cen * lax.rsqrt(var + EPS) * gln_ref[...] + bln_ref[...]
    y = y * _sigmoid(y)
    conv = _dot(y.astype(BF16), wpw_ref[...])
    att = att_ref[0].astype(BF16)
    merged = _dot(att, wout_ref[0:ATT_W, :]) + _dot(conv.astype(BF16), wout_ref[ATT_W:, :])
    x1 = x_ref[0] + gt1_ref[0] * merged
    x1_ref[0] = x1
    h2 = x1 * lax.rsqrt(jnp.mean(x1 * x1, axis=-1, keepdims=True) + EPS) * gffn_ref[...]
    h2 = h2 * (1.0 + sc2_ref[0]) + sh2_ref[0]
    hi, lo = _split_bf16(h2)
    h2_ref[0] = hi
    z = _dot(hi, wr_ref[...]) + _dot(lo, wr_ref[...])
    lg_ref[0] = z[:, :LANE] + z[:, LANE:] + br_ref[...]


def _post(x3, att3, u3, halo_src, gt1, sh2, sc2, wdw, bdw, gln, bln, wpw_b, wout_b, gffn, wr2, br,
          *, tm, seq_rows):
    nb, s, d = x3.shape
    nt = s // tm
    ch = u3.shape[-1]
    tile = lambda b, i: (b, i, 0)
    mod = lambda b, i: (b, 0, 0)
    const = lambda b, i: (0, 0)
    mod_rows = gt1.shape[1]
    if seq_rows is None:
        u_spec = pl.BlockSpec((1, tm, ch), tile)
        halo_spec = pl.BlockSpec((1, HALO, ch), lambda b, i: (b, jnp.maximum(i * (tm // HALO) - 1, 0), 0))
        scratch = [pltpu.VMEM((HALO + tm, ch), F32), pltpu.VMEM((7, HALO + tm, ch), F32)]
    else:
        u_spec = pl.BlockSpec(u3.shape, lambda b, i: (0, 0, 0))
        halo_spec = pl.BlockSpec(halo_src.shape, lambda b, i: (0, 0, 0))
        scratch = [pltpu.VMEM((HALO + seq_rows, ch), F32), pltpu.VMEM((tm, ch), F32)]
    return pl.pallas_call(
        functools.partial(_post_body, tm=tm, seq_rows=seq_rows),
        grid=(nb, nt),
        in_specs=[pl.BlockSpec((1, tm, d), tile),
                  pl.BlockSpec((1, tm, ATT_W), tile),
                  u_spec,
                  halo_spec,
                  pl.BlockSpec((1, mod_rows, d), mod),
                  pl.BlockSpec((1, mod_rows, d), mod),
                  pl.BlockSpec((1, mod_rows, d), mod),
                  pl.BlockSpec((CONV_K, ch), const),
                  pl.BlockSpec((1, ch), const),
                  pl.BlockSpec((1, ch), const),
                  pl.BlockSpec((1, ch), const),
                  pl.BlockSpec((ch, ch), const),
                  pl.BlockSpec((d, d), const),
                  pl.BlockSpec((1, d), const),
                  pl.BlockSpec((d, 2 * LANE), const),
                  pl.BlockSpec((1, LANE), const)],
        out_specs=[pl.BlockSpec((1, tm, d), tile),
                   pl.BlockSpec((1, tm, d), tile),
                   pl.BlockSpec((1, tm, LANE), tile)],
        out_shape=[jax.ShapeDtypeStruct((nb, s, d), F32),
                   jax.ShapeDtypeStruct((nb, s, d), BF16),
                   jax.ShapeDtypeStruct((nb, s, LANE), F32)],
        scratch_shapes=scratch,
        compiler_params=_cparams(("arbitrary", "arbitrary")),
        name="post_prompt" if seq_rows is None else "post_sample",
    )(x3, att3, u3, halo_src, gt1, sh2, sc2, wdw, bdw, gln, bln, wpw_b, wout_b, gffn, wr2, br)


def _combine_weights(lg):
    lane = lax.broadcasted_iota(jnp.int32, lg.shape, 1).astype(F32)
    big = 1e4
    gl = jnp.where(lane < N_GROUPS, lg, NEG)
    gmax = jnp.max(gl, axis=-1, keepdims=True)
    gsum = jnp.sum(jnp.exp(gl - gmax), axis=-1, keepdims=True)
    g_val = 1.0 / gsum
    g_idx = jnp.min(jnp.where(gl == gmax, lane, big), axis=-1, keepdims=True)
    lo = N_GROUPS + EXPERTS_PER_GROUP * g_idx
    el = jnp.where(lane >= lo, jnp.where(lane < lo + EXPERTS_PER_GROUP, lg, NEG), NEG)
    v1 = jnp.max(el, axis=-1, keepdims=True)
    i1 = jnp.min(jnp.where(el == v1, lane, big), axis=-1, keepdims=True)
    el2 = jnp.where(lane == i1, NEG, el)
    v2 = jnp.max(el2, axis=-1, keepdims=True)
    i2 = jnp.min(jnp.where(el2 == v2, lane, big), axis=-1, keepdims=True)
    e21 = jnp.exp(v2 - v1)
    w1 = g_val / (1.0 + e21)
    w2 = w1 * e21
    return jnp.where(lane == i1, w1, 0.0) + jnp.where(lane == i2, w2, 0.0), g_idx


def _dot_tn(a, b):
    return lax.dot_general(a, b, (((0,), (0,)), ((), ())), preferred_element_type=F32)


def _moe_body(x1_ref, h2_ref, lg_ref, gt2_ref, wg_ref, wu_ref, wd_ref, o_ref, xs_ref, cs_ref, ys_ref, off_ref, *, bm):
    tm = h2_ref.shape[1]
    comb, g_idx = _combine_weights(lg_ref[0])
    lane = lax.broadcasted_iota(jnp.int32, (tm, LANE), 1).astype(F32)
    onehot = jnp.where(lane == g_idx, 1.0, 0.0)
    r_i = lax.broadcasted_iota(jnp.int32, (tm, tm), 0)
    c_i = lax.broadcasted_iota(jnp.int32, (tm, tm), 1)
    earlier = _dot(jnp.where(c_i < r_i, 1.0, 0.0).astype(BF16), onehot.astype(BF16))
    totals = jnp.sum(onehot, axis=0, keepdims=True)
    lane1 = lane[0:1]
    starts = jnp.zeros((1, LANE), F32)
    run = jnp.zeros((1, 1), F32)
    off_ref[0] = 0
    for g in range(N_GROUPS):
        starts = starts + jnp.where(lane1 == g, run, 0.0)
        run = run + jnp.sum(jnp.where(lane1 == g, totals, 0.0), axis=-1, keepdims=True)
        off_ref[g + 1] = jnp.sum(run).astype(jnp.int32)
    pos = jnp.sum(onehot * (earlier + starts), axis=-1, keepdims=True)
    perm = jnp.where(pos == c_i.astype(F32), 1.0, 0.0).astype(BF16)
    xs_ref[...] = _dot_tn(perm, h2_ref[0]).astype(BF16)
    c_hl = _dot_tn(perm, jnp.concatenate(_split_bf16(comb), axis=1))
    cs_ref[...] = c_hl[:, :LANE] + c_hl[:, LANE:]
    ys_ref[...] = jnp.zeros(ys_ref.shape, F32)

    def block(k, carry):
        r0 = pl.multiple_of(k * bm, bm)

        def group(g, carry2):
            @pl.when((off_ref[g] < r0 + bm) & (off_ref[g + 1] > r0))
            def _():
                xb = xs_ref[pl.ds(r0, bm), :]
                cb = cs_ref[pl.ds(r0, bm), :]
                lane_b = lax.broadcasted_iota(jnp.int32, cb.shape, 1)
                yb = jnp.zeros((bm, ys_ref.shape[1]), F32)
                for ee in range(EXPERTS_PER_GROUP):
                    e = g * EXPERTS_PER_GROUP + ee
                    gate = _dot(xb, wg_ref[e])
                    up = _dot(xb, wu_ref[e])
                    ce = jnp.sum(jnp.where(lane_b == N_GROUPS + e, cb, 0.0), axis=-1, keepdims=True)
                    yb = yb + _dot((gate * _sigmoid(gate) * up * ce).astype(BF16), wd_ref[e])
                ys_ref[pl.ds(r0, bm), :] = ys_ref[pl.ds(r0, bm), :] + yb
            return carry2

        return lax.fori_loop(0, N_GROUPS, group, carry)

    lax.fori_loop(0, tm // bm, block, 0)
    y = _dot(perm, ys_ref[...].astype(BF16))
    o_ref[0] = x1_ref[0] + gt2_ref[0] * y


def _moe(x1, h2, lg, gt2, wg_b, wu_b, wd_b, *, tm):
    nb, s, d = x1.shape
    nt = s // tm
    mod_rows = gt2.shape[1]
    tile = lambda b, i: (b, i, 0)
    const3 = lambda b, i: (0, 0, 0)
    single = pl.Buffered(1)
    bm = min(tm, LANE)
    return pl.pallas_call(
        functools.partial(_moe_body, bm=bm),
        grid=(nb, nt),
        in_specs=[pl.BlockSpec((1, tm, d), tile),
                  pl.BlockSpec((1, tm, d), tile),
                  pl.BlockSpec((1, tm, LANE), tile),
                  pl.BlockSpec((1, mod_rows, d), lambda b, i: (b, 0, 0)),
                  pl.BlockSpec(wg_b.shape, const3, pipeline_mode=single),
                  pl.BlockSpec(wu_b.shape, const3, pipeline_mode=single),
                  pl.BlockSpec(wd_b.shape, const3, pipeline_mode=single)],
        out_specs=pl.BlockSpec((1, tm, d), tile),
        out_shape=jax.ShapeDtypeStruct((nb, s, d), F32),
        scratch_shapes=[pltpu.VMEM((tm, d), BF16),
                        pltpu.VMEM((tm, LANE), F32),
                        pltpu.VMEM((tm, d), F32),
                        pltpu.SMEM((N_GROUPS + 1,), jnp.int32)],
        compiler_params=_cparams(("arbitrary", "arbitrary")),
        name="moe",
    )(x1, h2, lg, gt2, wg_b, wu_b, wd_b)


def kernel(x_prompt, x_sample, c_prompt, c_sample, cache_k, cache_v, state_conv, page_table, w_ada, b_ada, g_norm_mix, w_in, g_q, g_k, lambda_q1, lambda_k1, lambda_q2, lambda_k2, g_attn_out, w_dw, b_dw, g_conv_ln, b_conv_ln, w_conv_pw, w_out, g_norm_ffn, w_router_group, b_router_group, w_router_expert, b_router_expert, w_exp_gate, w_exp_up, w_exp_down):
    depth = w_in.shape[0]
    bsz, seq, d = x_prompt.shape
    db, t_new, _ = x_sample.shape
    n_pages = page_table.shape[1]
    past = n_pages * PAGE
    n_phys = cache_k.shape[1]
    conv_ch = w_dw.shape[-1]
    t_att = min(512, seq)
    tm_post = min(512, seq)
    tm_moe = min(512, seq)
    n_s = db * t_new

    blk = jnp.arange(MXU) // QK_DIM
    pm = jnp.where(blk[:, None] == blk[None, :], 1.0 / QK_DIM, 0.0).astype(BF16)
    rows_q = 2 * N_HEADS * t_new
    row_head = jnp.arange(rows_q) // (2 * t_new)
    slope_col = jnp.asarray(ALIBI_SLOPES, F32)[row_head].reshape(rows_q, 1)
    qpos_col = (past + jnp.arange(rows_q) % t_new).astype(F32).reshape(rows_q, 1)
    cache_kt = jnp.transpose(cache_k, (0, 1, 3, 4, 5, 2)).reshape(depth, n_phys, Q_W, PAGE)
    cache_v4 = cache_v.reshape(depth, n_phys, PAGE * N_HEADS, V_DIM)

    yp = x_prompt
    ys = x_sample
    kp_l, vp_l, cp_l, ks_l, vs_l, cs_l = [], [], [], [], [], []
    for l in range(depth):
        lam_init = 0.8 - 0.6 * math.exp(-0.3 * l)
        lamv = jnp.stack([lambda_q1[l], lambda_k1[l], lambda_q2[l], lambda_k2[l]]).astype(F32)
        w_in_b = w_in[l].astype(BF16)
        wpw_b = w_conv_pw[l].astype(BF16)
        wout_b = w_out[l].astype(BF16)
        wg_b = w_exp_gate[l].astype(BF16)
        wu_b = w_exp_up[l].astype(BF16)
        wd_b = w_exp_down[l].astype(BF16)
        gq_t = jnp.tile(g_q[l], 2 * N_HEADS).reshape(1, Q_W)
        gk_t = jnp.tile(g_k[l], 2 * N_HEADS).reshape(1, Q_W)
        gmix = g_norm_mix[l].reshape(1, d)
        gffn = g_norm_ffn[l].reshape(1, d)
        wr = jnp.zeros((d, LANE), F32)
        wr = wr.at[:, :N_GROUPS].set(w_router_group[l]).at[:, N_GROUPS:N_GROUPS + N_EXPERTS].set(w_router_expert[l])
        wr_hi = wr.astype(BF16)
        wr2 = jnp.concatenate([wr_hi, (wr - wr_hi.astype(F32)).astype(BF16)], axis=1)
        br = jnp.zeros((1, LANE), F32)
        br = br.at[0, :N_GROUPS].set(b_router_group[l]).at[0, N_GROUPS:N_GROUPS + N_EXPERTS].set(b_router_expert[l])
        conv_w = (w_dw[l], b_dw[l].reshape(1, conv_ch), g_conv_ln[l].reshape(1, conv_ch),
                  b_conv_ln[l].reshape(1, conv_ch), wpw_b, wout_b, gffn, wr2, br)

        mod = _adaln(jnp.concatenate([c_prompt, c_sample], axis=0), w_ada[l], b_ada[l])
        mod = mod.reshape(bsz + db, 6, 1, d)
        modp = [mod[:bsz, j] for j in range(6)]
        mods = [mod[bsz:, j] for j in range(6)]

        kt, v4, up, qa, ka, vt = _proj(
            yp.reshape(bsz * seq, d), modp[0], modp[1], gmix, w_in_b, gq_t, gk_t, pm,
            tm=t_att, tiles_per_seq=seq // t_att, prompt=True)
        rep = lambda m: jnp.broadcast_to(m, (db, t_new, d)).reshape(1, n_s, d)
        qsm, ksm, vsm, usm = _proj(
            ys.reshape(n_s, d), rep(mods[0]), rep(mods[1]), gmix, w_in_b, gq_t, gk_t, pm,
            tm=n_s, tiles_per_seq=1, prompt=False)
        q5 = (qsm * ATT_SCALE).reshape(db, t_new, 2 * N_HEADS, QK_DIM)
        eye = jnp.eye(2 * N_HEADS, dtype=F32)
        wq = (q5.transpose(0, 2, 1, 3)[:, :, :, None, :] * eye[None, :, None, :, None])
        wq = wq.reshape(db, rows_q, Q_W)
        paged_args = (page_table, wq, ksm.reshape(db, t_new, Q_W), vsm.reshape(db, t_new, ATT_W), slope_col,
                      qpos_col, lamv, g_attn_out[l].reshape(1, V_DIM), cache_kt, cache_v4)

        attn_args = (qa.reshape(bsz, seq, 2 * Q_W), ka.reshape(bsz, seq, 2 * Q_W), vt, lamv,
                     g_attn_out[l].reshape(V_DIM, 1))
        cp_fused = math.gcd(n_pages, 8)
        if db * (n_pages // cp_fused) <= 2 * _page_slots(bsz, seq, t_att):
            att_p, att_s = _prompt_attention(*attn_args, t=t_att, lam_init=lam_init,
                                             paged_args=paged_args, cp=cp_fused, layer=l)
        else:
            att_p, _ = _prompt_attention(*attn_args, t=t_att, lam_init=lam_init)
            att_s = _sample_attention(*paged_args, cp=math.gcd(n_pages, 16), layer=l, lam_init=lam_init)

        up3 = up.reshape(bsz, seq, conv_ch)
        x1p, h2p, lgp = _post(yp, att_p, up3, up3, modp[2], modp[3], modp[4], *conv_w,
                              tm=tm_post, seq_rows=None)
        yp = _moe(x1p, h2p, lgp, modp[5], wg_b, wu_b, wd_b, tm=tm_moe)

        us3 = usm.reshape(db, t_new, conv_ch)
        state_pad = jnp.pad(state_conv[l], ((0, 0), (HALO - (CONV_K - 1), 0), (0, 0)))
        x1s, h2s, lgs = _post(ys.reshape(1, n_s, d), att_s.reshape(1, n_s, ATT_W), us3, state_pad,
                              rep(mods[2]), rep(mods[3]), rep(mods[4]), *conv_w, tm=n_s, seq_rows=t_new)
        ys = _moe(x1s, h2s, lgs, rep(mods[5]), wg_b, wu_b, wd_b, tm=n_s).reshape(db, t_new, d)

        kp_l.append(jnp.transpose(kt.reshape(bsz, N_HEADS, 2, QK_DIM, seq), (0, 4, 1, 2, 3)))
        vp_l.append(v4.reshape(bsz, seq, N_HEADS, V_DIM))
        cp_l.append(up3[:, seq - (CONV_K - 1):])
        ks_l.append(ksm.reshape(db, t_new, N_HEADS, 2, QK_DIM))
        vs_l.append(vsm.reshape(db, t_new, N_HEADS, V_DIM))
        cs_l.append(jnp.concatenate([state_conv[l], us3], axis=1)[:, t_new:])

    return (yp, ys, jnp.stack(kp_l), jnp.stack(vp_l), jnp.stack(cp_l),
            jnp.stack(ks_l), jnp.stack(vs_l), jnp.stack(cs_l))
```

```python
import functools
import math

import jax
import jax.numpy as jnp
from jax import lax
from jax.experimental import pallas as pl
from jax.experimental.pallas import tpu as pltpu

F32 = jnp.float32
BF16 = jnp.bfloat16

N_HEADS = 4
QK_DIM = 64
V_DIM = 128
ATT_W = N_HEADS * V_DIM
Q_W = N_HEADS * 2 * QK_DIM
CONV_K = 31
N_GROUPS = 4
EXPERTS_PER_GROUP = 4
N_EXPERTS = N_GROUPS * EXPERTS_PER_GROUP
EPS = 1e-6
ATT_SCALE = 1.0 / math.sqrt(QK_DIM)
PAGE = 128
NEG = -1e30
LOG2E = math.log2(math.e)
ALIBI_SLOPES = tuple(2.0 ** (-8.0 * (i + 1) / N_HEADS) for i in range(N_HEADS))

LANE = 128
MXU = 256
HALO = 32
VT_ROWS = V_DIM + 16
VMEM_LIMIT = 56 * 1024 * 1024


def _cparams(sem):
    return pltpu.CompilerParams(dimension_semantics=sem, vmem_limit_bytes=VMEM_LIMIT)


def _dot(a, b):
    return jnp.dot(a, b, preferred_element_type=F32)


def _dot_nt(a, b):
    return lax.dot_general(a, b, (((1,), (1,)), ((), ())), preferred_element_type=F32)


def _split_bf16(x):
    hi = x.astype(BF16)
    lo = (x - hi.astype(F32)).astype(BF16)
    return hi, lo


def _sigmoid(x):
    return 1.0 / (1.0 + jnp.exp(-x))


def _adaln_body(c_ref, w_ref, b_ref, o_ref):
    c = c_ref[...]
    o_ref[...] = jnp.dot(c * _sigmoid(c), w_ref[...], preferred_element_type=F32,
                         precision=lax.Precision.HIGHEST) + b_ref[...]


def _adaln(c, w_ada, b_ada):
    nb, d = c.shape
    n_out = w_ada.shape[1]
    bn = 1024
    return pl.pallas_call(
        _adaln_body,
        grid=(n_out // bn,),
        in_specs=[pl.BlockSpec((nb, d), lambda j: (0, 0)),
                  pl.BlockSpec((d, bn), lambda j: (0, j)),
                  pl.BlockSpec((1, bn), lambda j: (0, j))],
        out_specs=pl.BlockSpec((nb, bn), lambda j: (0, j)),
        out_shape=jax.ShapeDtypeStruct((nb, n_out), F32),
        compiler_params=_cparams(("arbitrary",)),
        name="adaln",
    )(c, w_ada, b_ada.reshape(1, n_out))


def _proj_body(x_ref, sh_ref, sc_ref, gmix_ref, w_ref, gq_ref, gk_ref, pm_ref, *out_refs, prompt, tiles_per_seq):
    x = x_ref[...]
    tm = x.shape[0]
    h = x * lax.rsqrt(jnp.mean(x * x, axis=-1, keepdims=True) + EPS) * gmix_ref[...]
    h = h * (1.0 + sc_ref[0]) + sh_ref[0]
    hb = h.astype(BF16)
    pm = pm_ref[...]

    def chunk_rms(z, g):
        outs = []
        for c in range(Q_W // MXU):
            zc = z[:, c * MXU:(c + 1) * MXU]
            hi, lo = _split_bf16(zc * zc)
            ms = _dot(hi, pm) + _dot(lo, pm)
            outs.append(zc * lax.rsqrt(ms + EPS))
        return jnp.concatenate(outs, axis=-1) * g

    qn = chunk_rms(_dot(hb, w_ref[:, 0:Q_W]), gq_ref[...])
    kn = chunk_rms(_dot(hb, w_ref[:, Q_W:2 * Q_W]), gk_ref[...])
    v = _dot(hb, w_ref[:, 2 * Q_W:2 * Q_W + ATT_W])
    a = _dot(hb, w_ref[:, 2 * Q_W + ATT_W:])
    cw = a.shape[-1] // 2
    u = a[:, :cw] * _sigmoid(a[:, cw:])
    if not prompt:
        q_ref, k_ref, v_ref, u_ref = out_refs
        q_ref[...] = qn
        k_ref[...] = kn
        v_ref[...] = v
        u_ref[...] = u
        return

    kt_ref, v4_ref, u_ref, qa_ref, ka_ref, vt_ref = out_refs
    u_ref[...] = u
    kt_ref[0] = kn.T
    for hh in range(N_HEADS):
        v4_ref[pl.ds(hh, tm, stride=N_HEADS), :] = v[:, hh * V_DIM:(hh + 1) * V_DIM]
    vt = v.T
    for hh in range(N_HEADS):
        vt_ref[0, 0, hh, 0:V_DIM, :] = vt[hh * V_DIM:(hh + 1) * V_DIM, :].astype(BF16)
        vt_ref[0, 0, hh, V_DIM:VT_ROWS, :] = jnp.ones((VT_ROWS - V_DIM, tm), BF16)
    lane = lax.broadcasted_iota(jnp.int32, (tm, LANE), 1)
    pos = ((pl.program_id(0) % tiles_per_seq) * tm + lax.broadcasted_iota(jnp.int32, (tm, 1), 0)).astype(F32)
    q_aux = jnp.where(lane < QK_DIM + 3, 1.0, 0.0)
    qs = qn * (ATT_SCALE * LOG2E)
    for hh in range(N_HEADS):
        b = pos * (ALIBI_SLOPES[hh] * LOG2E)
        b_hi = b.astype(BF16).astype(F32)
        b_mid = (b - b_hi).astype(BF16).astype(F32)
        b_lo = b - b_hi - b_mid
        k_aux = jnp.where(lane == QK_DIM, b_hi,
                          jnp.where(lane == QK_DIM + 1, b_mid, jnp.where(lane == QK_DIM + 2, b_lo, 0.0)))
        qg = qs[:, hh * LANE:(hh + 1) * LANE]
        kg = kn[:, hh * LANE:(hh + 1) * LANE]
        for mp in range(2):
            if mp == 1:
                qg = pltpu.roll(qg, QK_DIM, 1)
                kg = pltpu.roll(kg, QK_DIM, 1)
            c0 = (2 * hh + mp) * LANE
            qa_ref[:, c0:c0 + LANE] = jnp.where(lane < QK_DIM, qg, q_aux).astype(BF16)
            ka_ref[:, c0:c0 + LANE] = jnp.where(lane < QK_DIM, kg, k_aux).astype(BF16)


def _proj(x2, sh, sc, gmix, w_in_b, gq_t, gk_t, pm, *, tm, tiles_per_seq, prompt):
    n, d = x2.shape
    nt = n // tm
    in_w = w_in_b.shape[1]
    conv_ch = (in_w - 2 * Q_W - ATT_W) // 2
    mod_rows = sh.shape[1]
    row = lambda i: (i, 0)
    const = lambda i: (0, 0)
    mod_idx = lambda i: (i // tiles_per_seq, 0, 0)
    in_specs = [pl.BlockSpec((tm, d), row),
                pl.BlockSpec((1, mod_rows, d), mod_idx),
                pl.BlockSpec((1, mod_rows, d), mod_idx),
                pl.BlockSpec((1, d), const),
                pl.BlockSpec((d, in_w), const),
                pl.BlockSpec((1, Q_W), const),
                pl.BlockSpec((1, Q_W), const),
                pl.BlockSpec((MXU, MXU), const)]
    if prompt:
        nb = nt // tiles_per_seq
        out_shape = [jax.ShapeDtypeStruct((nb, Q_W, tiles_per_seq * tm), F32),
                     jax.ShapeDtypeStruct((n * N_HEADS, V_DIM), F32),
                     jax.ShapeDtypeStruct((n, conv_ch), F32),
                     jax.ShapeDtypeStruct((n, 2 * Q_W), BF16),
                     jax.ShapeDtypeStruct((n, 2 * Q_W), BF16),
                     jax.ShapeDtypeStruct((nb, tiles_per_seq, N_HEADS, VT_ROWS, tm), BF16)]
        out_specs = [pl.BlockSpec((1, Q_W, tm), lambda i: (i // tiles_per_seq, 0, i % tiles_per_seq)),
                     pl.BlockSpec((tm * N_HEADS, V_DIM), row),
                     pl.BlockSpec((tm, conv_ch), row),
                     pl.BlockSpec((tm, 2 * Q_W), row),
                     pl.BlockSpec((tm, 2 * Q_W), row),
                     pl.BlockSpec((1, 1, N_HEADS, VT_ROWS, tm),
                                  lambda i: (i // tiles_per_seq, i % tiles_per_seq, 0, 0, 0))]
    else:
        f32_out = jax.ShapeDtypeStruct((n, Q_W), F32)
        f32_spec = pl.BlockSpec((tm, Q_W), row)
        out_shape = [f32_out, f32_out, f32_out, jax.ShapeDtypeStruct((n, conv_ch), F32)]
        out_specs = [f32_spec, f32_spec, f32_spec, pl.BlockSpec((tm, conv_ch), row)]
    return pl.pallas_call(
        functools.partial(_proj_body, prompt=prompt, tiles_per_seq=tiles_per_seq),
        grid=(nt,),
        in_specs=in_specs,
        out_specs=out_specs,
        out_shape=out_shape,
        compiler_params=_cparams(("arbitrary",)),
        name="proj_prompt" if prompt else "proj_sample",
    )(x2, sh, sc, gmix, w_in_b, gq_t, gk_t, pm)


def _lambda(lamv, lam_init):
    a = jnp.sum(lamv[0:1] * lamv[1:2], axis=-1, keepdims=True)
    b = jnp.sum(lamv[2:3] * lamv[3:4], axis=-1, keepdims=True)
    return jnp.exp(a) - jnp.exp(b) + lam_init


def _pattn_body(*refs, t, lam_init, paged):
    if paged is None:
        (q_ref, k_ref, vt_ref, lamv_ref, gatt_ref, o_ref,
         m_ref, acc_ref, s0_ref, s1_ref, mb0_ref, mb1_ref) = refs

        def page_phase(idx, which):
            pass
    else:
        pt_ref, q_ref, k_ref, vt_ref, lamv_ref, gatt_ref = refs[:6]
        o_ref, os_ref, m_ref, acc_ref, s0_ref, s1_ref, mb0_ref, mb1_ref = refs[15:23]
        static = dict(paged)
        per_slot = static.pop("per_slot")
        paged_exact = static.pop("exact")
        nq = static.pop("nq")
        prime, phases = _paged_chunk_fns(pt_ref, *refs[6:15], os_ref, *refs[23:], lam_init=lam_init, **static)
        first = (pl.program_id(0) == 0) & (pl.program_id(1) == 0) & (pl.program_id(2) == 0)

        @pl.when(first)
        def _():
            prime()

        qi = pl.program_id(2)
        slots_bh = (nq * nq) // 4
        slot0 = (pl.program_id(0) * N_HEADS + pl.program_id(1)) * slots_bh + (qi * qi) // 4

        def page_phase(idx, which):
            if per_slot == 1 and paged_exact:
                phases[which](slot0 + idx)
            elif which == len(phases) - 1:
                for r in range(per_slot):
                    g = (slot0 + idx) * per_slot + r

                    @pl.when(g < static["total"])
                    def _():
                        for phase in phases:
                            phase(g)

    i = pl.program_id(2)
    m_ref[...] = jnp.full(m_ref.shape, NEG, F32)
    acc_ref[...] = jnp.zeros(acc_ref.shape, F32)

    def scores(j, s_ref, mb_ref):
        kblk = k_ref[0, pl.ds(pl.multiple_of(j * t, t), t), :]
        for mp in range(2):
            sl = slice(mp * t, (mp + 1) * t)
            s = _dot_nt(kblk[:, mp * LANE:(mp + 1) * LANE], q_ref[0, :, mp * LANE:(mp + 1) * LANE])
            s_ref[:, sl] = s
            mb_ref[:, sl] = jnp.max(s, axis=0, keepdims=True)

    def accumulate(j, s_ref, mb_ref, masked):
        vt = vt_ref[0, j, 0]
        for mp in range(2):
            sl = slice(mp * t, (mp + 1) * t)
            s = s_ref[:, sl]
            if masked:
                r = lax.broadcasted_iota(jnp.int32, s.shape, 0)
                c = lax.broadcasted_iota(jnp.int32, s.shape, 1)
                s = jnp.where(r <= c, s, NEG)
                mb = jnp.max(s, axis=0, keepdims=True)
            else:
                mb = mb_ref[:, sl]
            m_old = m_ref[:, sl]
            m_new = jnp.maximum(m_old, mb)
            alpha = jnp.exp2(m_old - m_new)
            p = jnp.exp2(s - m_new).astype(BF16)
            acc_ref[:, sl] = alpha * acc_ref[:, sl] + _dot(vt, p)
            m_ref[:, sl] = m_new

    scores(0, s0_ref, mb0_ref)

    def body(jj, carry):
        j = 2 * jj
        page_phase(jj, 0)
        scores(j + 1, s1_ref, mb1_ref)
        page_phase(jj, 1)
        accumulate(j, s0_ref, mb0_ref, False)
        scores(j + 2, s0_ref, mb0_ref)
        accumulate(j + 1, s1_ref, mb1_ref, False)
        page_phase(jj, 2)
        page_phase(jj, 3)
        return carry

    lax.fori_loop(0, i // 2, body, 0)

    @pl.when(i % 2 == 0)
    def _():
        accumulate(i, s0_ref, mb0_ref, True)

    @pl.when(i % 2 == 1)
    def _():
        page_phase(i // 2, 0)
        scores(i, s1_ref, mb1_ref)
        page_phase(i // 2, 1)
        accumulate(i - 1, s0_ref, mb0_ref, False)
        accumulate(i, s1_ref, mb1_ref, True)
        page_phase(i // 2, 2)
        page_phase(i // 2, 3)

    acc = acc_ref[...]
    o = acc[0:V_DIM] * (1.0 / acc[V_DIM:V_DIM + 1])
    lam = _lambda(lamv_ref[...], lam_init)
    d = o[:, :t] - lam * o[:, t:]
    ms = jnp.mean(d * d, axis=0, keepdims=True)
    out = d * lax.rsqrt(ms + EPS) * (gatt_ref[...] * (1.0 - lam_init))
    o_ref[0] = out.T.astype(BF16)


def _page_slots(bsz, seq, t):
    nq = seq // t
    return bsz * N_HEADS * ((nq * nq) // 4)


def _prompt_attention(qa, ka, vt, lamv, gatt_col, *, t, lam_init, paged_args=None, cp=None, layer=0):
    b, s, _ = qa.shape
    nq = s // t
    in_specs = [pl.BlockSpec((1, t, 2 * LANE), lambda bb, h, i, *_: (bb, i, h)),
                pl.BlockSpec((1, s, 2 * LANE), lambda bb, h, i, *_: (bb, 0, h)),
                pl.BlockSpec((1, nq, 1, VT_ROWS, t), lambda bb, h, i, *_: (bb, 0, h, 0, 0)),
                pl.BlockSpec((4, QK_DIM), lambda bb, h, i, *_: (0, 0)),
                pl.BlockSpec((V_DIM, 1), lambda bb, h, i, *_: (0, 0))]
    out_specs = [pl.BlockSpec((1, t, LANE), lambda bb, h, i, *_: (bb, i, h))]
    out_shape = [jax.ShapeDtypeStruct((b, s, ATT_W), BF16)]
    scratch = [pltpu.VMEM((1, 2 * t), F32),
               pltpu.VMEM((VT_ROWS, 2 * t), F32),
               pltpu.VMEM((t, 2 * t), F32),
               pltpu.VMEM((t, 2 * t), F32),
               pltpu.VMEM((1, 2 * t), F32),
               pltpu.VMEM((1, 2 * t), F32)]
    args = (qa, ka, vt, lamv, gatt_col)
    paged = None
    if paged_args is not None:
        page_table, wq, k_new, v_new = paged_args[:4]
        db, t_new, _ = k_new.shape
        nc = page_table.shape[1] // cp
        total = db * nc
        p_in, p_out, p_scratch = _paged_specs(wq, k_new, v_new, cp, lambda nd: (lambda bb, h, i, pt: (0,) * nd))
        in_specs += p_in
        out_specs.append(p_out)
        out_shape.append(jax.ShapeDtypeStruct((db, t_new, ATT_W), F32))
        scratch += p_scratch
        args = (page_table,) + args + tuple(paged_args[1:])
        paged = (("cp", cp), ("nc", nc), ("total", total), ("layer", layer), ("t_new", t_new),
                 ("per_slot", -(-total // _page_slots(b, s, t))), ("exact", total == _page_slots(b, s, t)),
                 ("nq", nq))
    outs = pl.pallas_call(
        functools.partial(_pattn_body, t=t, lam_init=lam_init, paged=paged),
        grid_spec=pltpu.PrefetchScalarGridSpec(
            num_scalar_prefetch=0 if paged is None else 1, grid=(b, N_HEADS, nq),
            in_specs=in_specs, out_specs=out_specs, scratch_shapes=scratch),
        out_shape=out_shape,
        compiler_params=_cparams(("arbitrary", "arbitrary", "arbitrary")),
        name="prompt_attention",
    )(*args)
    return (outs[0], outs[1]) if paged is not None else (outs[0], None)


def _paged_chunk_fns(pt_ref, wq_ref, knew_ref, vnew_ref, slope_ref, qpos_ref, lamv_ref, gatt_ref,
                     ck_hbm, cv_hbm, o_ref, kbuf, vbuf, sem, m_ref, l_ref, acc_ref, pad_ref, p_ref, al_ref,
                     *, cp, nc, total, layer, t_new, lam_init):
    keys = cp * PAGE
    hrows = 2 * t_new

    def copies(bb, cc, sl):
        out = []
        for p in range(cp):
            page = pt_ref[bb, cc * cp + p]
            out.append(pltpu.make_async_copy(ck_hbm.at[layer, page], kbuf.at[sl, :, pl.ds(p * PAGE, PAGE)],
                                             sem.at[0, sl]))
            out.append(pltpu.make_async_copy(cv_hbm.at[layer, page],
                                             vbuf.at[sl, pl.ds(p * PAGE * N_HEADS, PAGE * N_HEADS), :],
                                             sem.at[1, sl]))
        return out

    def prime():
        for cpy in copies(0, 0, 0):
            cpy.start()

    def probs(s):
        m_old = m_ref[...]
        m_new = jnp.maximum(m_old, jnp.max(s, axis=-1, keepdims=True))
        alpha = jnp.exp(m_old - m_new)
        p = jnp.exp(s - m_new)
        l_ref[...] = alpha * l_ref[...] + jnp.sum(p, axis=-1, keepdims=True)
        m_ref[...] = m_new
        return p, alpha

    def weigh(p, alpha, head_values):
        for hh in range(N_HEADS):
            rs = slice(hh * hrows, (hh + 1) * hrows)
            acc_ref[rs, :] = alpha[rs] * acc_ref[rs, :] + _dot(p[rs], head_values(hh))

    def top(g):
        slot = g % 2

        @pl.when(g + 1 < total)
        def _():
            gn = g + 1
            for cpy in copies(gn // nc, gn % nc, 1 - slot):
                cpy.start()

        @pl.when(g % nc == 0)
        def _():
            m_ref[...] = jnp.full(m_ref.shape, NEG, F32)
            l_ref[...] = jnp.zeros(l_ref.shape, F32)
            acc_ref[...] = jnp.zeros(acc_ref.shape, F32)

        for cpy in copies(g // nc, g % nc, slot):
            cpy.wait()

    def qk(g):
        c = g % nc
        kpos = (c * keys + lax.broadcasted_iota(jnp.int32, (1, keys), 1)).astype(F32)
        s = _dot(wq_ref[g // nc], kbuf[g % 2]) - slope_ref[...] * (qpos_ref[...] - kpos)
        p_ref[...], al_ref[...] = probs(s)

    def pv(g):
        vslot = vbuf.at[g % 2]
        weigh(p_ref[...], al_ref[...], lambda hh: vslot[pl.ds(hh, keys, stride=N_HEADS), :])

    def bottom(g):
        b = g // nc

        @pl.when(g % nc == nc - 1)
        def _():
            wq = wq_ref[b]
            slope = slope_ref[...]
            pad_ref[...] = jnp.zeros(pad_ref.shape, F32)
            pad_ref[0:t_new, :] = knew_ref[b]
            s2 = _dot_nt(wq, pad_ref[...])
            tk = lax.broadcasted_iota(jnp.int32, s2.shape, 1)
            tq = lax.broadcasted_iota(jnp.int32, s2.shape, 0) % t_new
            s2 = jnp.where(tk <= tq, s2 - slope * (tq - tk).astype(F32), NEG)
            pad_ref[0:t_new, :] = vnew_ref[b]
            p2, alpha2 = probs(s2)
            weigh(p2, alpha2, lambda hh: pad_ref[:, hh * V_DIM:(hh + 1) * V_DIM])

            o = acc_ref[...] * (1.0 / l_ref[...])
            lam = _lambda(lamv_ref[...], lam_init)
            gatt = gatt_ref[...] * (1.0 - lam_init)
            for hh in range(N_HEADS):
                r0 = hh * hrows
                d = o[r0:r0 + t_new] - lam * o[r0 + t_new:r0 + hrows]
                ms = jnp.mean(d * d, axis=-1, keepdims=True)
                o_ref[b, :, hh * V_DIM:(hh + 1) * V_DIM] = d * lax.rsqrt(ms + EPS) * gatt

    return prime, (top, qk, pv, bottom)


def _paged_specs(wq, k_new, v_new, cp, index_map):
    db, t_new, _ = k_new.shape
    rows_q = wq.shape[1]
    in_specs = [pl.BlockSpec(wq.shape, index_map(3)),
                pl.BlockSpec(k_new.shape, index_map(3)),
                pl.BlockSpec(v_new.shape, index_map(3)),
                pl.BlockSpec((rows_q, 1), index_map(2)),
                pl.BlockSpec((rows_q, 1), index_map(2)),
                pl.BlockSpec((4, QK_DIM), index_map(2)),
                pl.BlockSpec((1, V_DIM), index_map(2)),
                pl.BlockSpec(memory_space=pl.ANY),
                pl.BlockSpec(memory_space=pl.ANY)]
    out_spec = pl.BlockSpec((db, t_new, ATT_W), index_map(3))
    scratch = [pltpu.VMEM((2, Q_W, cp * PAGE), F32),
               pltpu.VMEM((2, cp * PAGE * N_HEADS, V_DIM), F32),
               pltpu.SemaphoreType.DMA((2, 2)),
               pltpu.VMEM((rows_q, 1), F32),
               pltpu.VMEM((rows_q, 1), F32),
               pltpu.VMEM((rows_q, V_DIM), F32),
               pltpu.VMEM((PAGE, Q_W), F32),
               pltpu.VMEM((rows_q, cp * PAGE), F32),
               pltpu.VMEM((rows_q, 1), F32)]
    return in_specs, out_spec, scratch


def _sattn_body(pt_ref, *refs, **static):
    prime, phases = _paged_chunk_fns(pt_ref, *refs, **static)
    g = pl.program_id(0)

    @pl.when(g == 0)
    def _():
        prime()

    for phase in phases:
        phase(g)


def _sample_attention(page_table, wq, k_new, v_new, slope_col, qpos_col, lamv, gatt_row,
                      cache_kt, cache_v4, *, cp, layer, lam_init):
    db, t_new, _ = k_new.shape
    nc = page_table.shape[1] // cp
    total = db * nc
    in_specs, out_spec, scratch = _paged_specs(wq, k_new, v_new, cp, lambda nd: (lambda g, pt: (0,) * nd))
    return pl.pallas_call(
        functools.partial(_sattn_body, cp=cp, nc=nc, total=total, layer=layer, t_new=t_new, lam_init=lam_init),
        grid_spec=pltpu.PrefetchScalarGridSpec(
            num_scalar_prefetch=1, grid=(total,), in_specs=in_specs, out_specs=out_spec, scratch_shapes=scratch),
        out_shape=jax.ShapeDtypeStruct((db, t_new, ATT_W), F32),
        compiler_params=_cparams(("arbitrary",)),
        name="sample_attention",
    )(page_table, wq, k_new, v_new, slope_col, qpos_col, lamv, gatt_row, cache_kt, cache_v4)


def _post_body(x_ref, att_ref, u_ref, halo_ref, gt1_ref, sh2_ref, sc2_ref, wdw_ref, bdw_ref, gln_ref, bln_ref,
               wpw_ref, wout_ref, gffn_ref, wr_ref, br_ref, x1_ref, h2_ref, lg_ref, useq_ref, aux_ref,
               *, tm, seq_rows):
    wdw = wdw_ref[...]
    off = HALO - (CONV_K - 1)

    def taps(rows, read):
        acc = jnp.zeros((rows, wdw.shape[1]), F32) + bdw_ref[...]
        for j in range(CONV_K):
            acc = acc + wdw[j:j + 1, :] * read(off + j)
        return acc

    if seq_rows is None:
        halo = halo_ref[0]
        halo = jnp.where(pl.program_id(1) > 0, halo, jnp.zeros_like(halo))
        useq_ref[0:HALO, :] = halo
        useq_ref[HALO:HALO + tm, :] = u_ref[0]
        rows = tm + HALO - 8
        for s in range(1, 8):
            aux_ref[s - 1, 0:rows, :] = useq_ref[s:s + rows, :]

        def read(o):
            s = o % 8
            src = useq_ref if s == 0 else aux_ref.at[s - 1]
            return src[o - s:o - s + tm, :]

        acc = taps(tm, read)
    else:
        def one_seq(b, carry):
            useq_ref[0:HALO, :] = halo_ref[b]
            useq_ref[HALO:HALO + seq_rows, :] = u_ref[b]
            aux_ref[pl.ds(pl.multiple_of(b * seq_rows, seq_rows), seq_rows), :] = taps(
                seq_rows, lambda o: useq_ref[o:o + seq_rows, :])
            return carry

        lax.fori_loop(0, tm // seq_rows, one_seq, 0)
        acc = aux_ref[...]
    mu = jnp.mean(acc, axis=-1, keepdims=True)
    cen = acc - mu
    var = jnp.mean(cen * cen, axis=-1, keepdims=True)
    y = cen * lax.rsqrt(var + EPS) * gln_ref[...] + bln_ref[...]
    y = y * _sigmoid(y)
    conv = _dot(y.astype(BF16), wpw_ref[...])
    att = att_ref[0].astype(BF16)
    merged = _dot(att, wout_ref[0:ATT_W, :]) + _dot(conv.astype(BF16), wout_ref[ATT_W:, :])
    x1 = x_ref[0] + gt1_ref[0] * merged
    x1_ref[0] = x1
    h2 = x1 * lax.rsqrt(jnp.mean(x1 * x1, axis=-1, keepdims=True) + EPS) * gffn_ref[...]
    h2 = h2 * (1.0 + sc2_ref[0]) + sh2_ref[0]
    hi, lo = _split_bf16(h2)
    h2_ref[0] = hi
    z = _dot(hi, wr_ref[...]) + _dot(lo, wr_ref[...])
    lg_ref[0] = z[:, :LANE] + z[:, LANE:] + br_ref[...]


def _post(x3, att3, u3, halo_src, gt1, sh2, sc2, wdw, bdw, gln, bln, wpw_b, wout_b, gffn, wr2, br,
          *, tm, seq_rows):
    nb, s, d = x3.shape
    nt = s // tm
    ch = u3.shape[-1]
    tile = lambda b, i: (b, i, 0)
    mod = lambda b, i: (b, 0, 0)
    const = lambda b, i: (0, 0)
    mod_rows = gt1.shape[1]
    if seq_rows is None:
        u_spec = pl.BlockSpec((1, tm, ch), tile)
        halo_spec = pl.BlockSpec((1, HALO, ch), lambda b, i: (b, jnp.maximum(i * (tm // HALO) - 1, 0), 0))
        scratch = [pltpu.VMEM((HALO + tm, ch), F32), pltpu.VMEM((7, HALO + tm, ch), F32)]
    else:
        u_spec = pl.BlockSpec(u3.shape, lambda b, i: (0, 0, 0))
        halo_spec = pl.BlockSpec(halo_src.shape, lambda b, i: (0, 0, 0))
        scratch = [pltpu.VMEM((HALO + seq_rows, ch), F32), pltpu.VMEM((tm, ch), F32)]
    return pl.pallas_call(
        functools.partial(_post_body, tm=tm, seq_rows=seq_rows),
        grid=(nb, nt),
        in_specs=[pl.BlockSpec((1, tm, d), tile),
                  pl.BlockSpec((1, tm, ATT_W), tile),
                  u_spec,
                  halo_spec,
                  pl.BlockSpec((1, mod_rows, d), mod),
                  pl.BlockSpec((1, mod_rows, d), mod),
                  pl.BlockSpec((1, mod_rows, d), mod),
                  pl.BlockSpec((CONV_K, ch), const),
                  pl.BlockSpec((1, ch), const),
                  pl.BlockSpec((1, ch), const),
                  pl.BlockSpec((1, ch), const),
                  pl.BlockSpec((ch, ch), const),
                  pl.BlockSpec((d, d), const),
                  pl.BlockSpec((1, d), const),
                  pl.BlockSpec((d, 2 * LANE), const),
                  pl.BlockSpec((1, LANE), const)],
        out_specs=[pl.BlockSpec((1, tm, d), tile),
                   pl.BlockSpec((1, tm, d), tile),
                   pl.BlockSpec((1, tm, LANE), tile)],
        out_shape=[jax.ShapeDtypeStruct((nb, s, d), F32),
                   jax.ShapeDtypeStruct((nb, s, d), BF16),
                   jax.ShapeDtypeStruct((nb, s, LANE), F32)],
        scratch_shapes=scratch,
        compiler_params=_cparams(("arbitrary", "arbitrary")),
        name="post_prompt" if seq_rows is None else "post_sample",
    )(x3, att3, u3, halo_src, gt1, sh2, sc2, wdw, bdw, gln, bln, wpw_b, wout_b, gffn, wr2, br)


def _combine_weights(lg):
    lane = lax.broadcasted_iota(jnp.int32, lg.shape, 1).astype(F32)
    big = 1e4
    gl = jnp.where(lane < N_GROUPS, lg, NEG)
    gmax = jnp.max(gl, axis=-1, keepdims=True)
    gsum = jnp.sum(jnp.exp(gl - gmax), axis=-1, keepdims=True)
    g_val = 1.0 / gsum
    g_idx = jnp.min(jnp.where(gl == gmax, lane, big), axis=-1, keepdims=True)
    lo = N_GROUPS + EXPERTS_PER_GROUP * g_idx
    el = jnp.where(lane >= lo, jnp.where(lane < lo + EXPERTS_PER_GROUP, lg, NEG), NEG)
    v1 = jnp.max(el, axis=-1, keepdims=True)
    i1 = jnp.min(jnp.where(el == v1, lane, big), axis=-1, keepdims=True)
    el2 = jnp.where(lane == i1, NEG, el)
    v2 = jnp.max(el2, axis=-1, keepdims=True)
    i2 = jnp.min(jnp.where(el2 == v2, lane, big), axis=-1, keepdims=True)
    e21 = jnp.exp(v2 - v1)
    w1 = g_val / (1.0 + e21)
    w2 = w1 * e21
    return jnp.where(lane == i1, w1, 0.0) + jnp.where(lane == i2, w2, 0.0), g_idx


def _dot_tn(a, b):
    return lax.dot_general(a, b, (((0,), (0,)), ((), ())), preferred_element_type=F32)


def _moe_body(x1_ref, h2_ref, lg_ref, gt2_ref, wgu_ref, wd_ref, o_ref, xs_ref, cs_ref, ys_ref, off_ref, *, bm):
    tm = h2_ref.shape[1]
    comb, g_idx = _combine_weights(lg_ref[0])
    lane = lax.broadcasted_iota(jnp.int32, (tm, LANE), 1).astype(F32)
    onehot = jnp.where(lane == g_idx, 1.0, 0.0)
    r_i = lax.broadcasted_iota(jnp.int32, (tm, tm), 0)
    c_i = lax.broadcasted_iota(jnp.int32, (tm, tm), 1)
    earlier = _dot(jnp.where(c_i < r_i, 1.0, 0.0).astype(BF16), onehot.astype(BF16))
    totals = jnp.sum(onehot, axis=0, keepdims=True)
    lane1 = lane[0:1]
    starts = jnp.zeros((1, LANE), F32)
    run = jnp.zeros((1, 1), F32)
    off_ref[0] = 0
    for g in range(N_GROUPS):
        starts = starts + jnp.where(lane1 == g, run, 0.0)
        run = run + jnp.sum(jnp.where(lane1 == g, totals, 0.0), axis=-1, keepdims=True)
        off_ref[g + 1] = jnp.sum(run).astype(jnp.int32)
    pos = jnp.sum(onehot * (earlier + starts), axis=-1, keepdims=True)
    perm = jnp.where(pos == c_i.astype(F32), 1.0, 0.0).astype(BF16)
    xs_ref[...] = _dot_tn(perm, h2_ref[0]).astype(BF16)
    c_hl = _dot_tn(perm, jnp.concatenate(_split_bf16(comb), axis=1))
    cs_ref[...] = c_hl[:, :LANE] + c_hl[:, LANE:]
    ys_ref[...] = jnp.zeros(ys_ref.shape, F32)

    def block(k, carry):
        r0 = pl.multiple_of(k * bm, bm)

        def group(g, carry2):
            @pl.when((off_ref[g] < r0 + bm) & (off_ref[g + 1] > r0))
            def _():
                xb = xs_ref[pl.ds(r0, bm), :]
                cb = cs_ref[pl.ds(r0, bm), :]
                lane_b = lax.broadcasted_iota(jnp.int32, cb.shape, 1)
                gu = _dot(xb, wgu_ref[g])
                half = gu.shape[1] // 2
                de = half // EXPERTS_PER_GROUP
                gate = gu[:, :half]
                ce = jnp.concatenate(
                    [jnp.broadcast_to(jnp.sum(jnp.where(lane_b == N_GROUPS + g * EXPERTS_PER_GROUP + ee, cb, 0.0),
                                              axis=-1, keepdims=True), (bm, de))
                     for ee in range(EXPERTS_PER_GROUP)], axis=1)
                hid = (gate * _sigmoid(gate) * gu[:, half:] * ce).astype(BF16)
                ys_ref[pl.ds(r0, bm), :] = ys_ref[pl.ds(r0, bm), :] + _dot(hid, wd_ref[g])
            return carry2

        return lax.fori_loop(0, N_GROUPS, group, carry)

    lax.fori_loop(0, tm // bm, block, 0)
    y = _dot(perm, ys_ref[...].astype(BF16))
    o_ref[0] = x1_ref[0] + gt2_ref[0] * y


def _moe(x1, h2, lg, gt2, wgu_b, wd_b, *, tm):
    nb, s, d = x1.shape
    nt = s // tm
    mod_rows = gt2.shape[1]
    tile = lambda b, i: (b, i, 0)
    const3 = lambda b, i: (0, 0, 0)
    single = pl.Buffered(1)
    bm = min(tm, LANE)
    return pl.pallas_call(
        functools.partial(_moe_body, bm=bm),
        grid=(nb, nt),
        in_specs=[pl.BlockSpec((1, tm, d), tile),
                  pl.BlockSpec((1, tm, d), tile),
                  pl.BlockSpec((1, tm, LANE), tile),
                  pl.BlockSpec((1, mod_rows, d), lambda b, i: (b, 0, 0)),
                  pl.BlockSpec(wgu_b.shape, const3, pipeline_mode=single),
                  pl.BlockSpec(wd_b.shape, const3, pipeline_mode=single)],
        out_specs=pl.BlockSpec((1, tm, d), tile),
        out_shape=jax.ShapeDtypeStruct((nb, s, d), F32),
        scratch_shapes=[pltpu.VMEM((tm, d), BF16),
                        pltpu.VMEM((tm, LANE), F32),
                        pltpu.VMEM((tm, d), F32),
                        pltpu.SMEM((N_GROUPS + 1,), jnp.int32)],
        compiler_params=_cparams(("arbitrary", "arbitrary")),
        name="moe",
    )(x1, h2, lg, gt2, wgu_b, wd_b)


def kernel(x_prompt, x_sample, c_prompt, c_sample, cache_k, cache_v, state_conv, page_table, w_ada, b_ada, g_norm_mix, w_in, g_q, g_k, lambda_q1, lambda_k1, lambda_q2, lambda_k2, g_attn_out, w_dw, b_dw, g_conv_ln, b_conv_ln, w_conv_pw, w_out, g_norm_ffn, w_router_group, b_router_group, w_router_expert, b_router_expert, w_exp_gate, w_exp_up, w_exp_down):
    depth = w_in.shape[0]
    bsz, seq, d = x_prompt.shape
    db, t_new, _ = x_sample.shape
    n_pages = page_table.shape[1]
    past = n_pages * PAGE
    n_phys = cache_k.shape[1]
    conv_ch = w_dw.shape[-1]
    t_att = min(512, seq)
    tm_post = min(512, seq)
    tm_moe = min(512, seq)
    n_s = db * t_new

    blk = jnp.arange(MXU) // QK_DIM
    pm = jnp.where(blk[:, None] == blk[None, :], 1.0 / QK_DIM, 0.0).astype(BF16)
    rows_q = 2 * N_HEADS * t_new
    row_head = jnp.arange(rows_q) // (2 * t_new)
    slope_col = jnp.asarray(ALIBI_SLOPES, F32)[row_head].reshape(rows_q, 1)
    qpos_col = (past + jnp.arange(rows_q) % t_new).astype(F32).reshape(rows_q, 1)
    cache_kt = jnp.transpose(cache_k, (0, 1, 3, 4, 5, 2)).reshape(depth, n_phys, Q_W, PAGE)
    cache_v4 = cache_v.reshape(depth, n_phys, PAGE * N_HEADS, V_DIM)

    yp = x_prompt
    ys = x_sample
    kp_l, vp_l, cp_l, ks_l, vs_l, cs_l = [], [], [], [], [], []
    for l in range(depth):
        lam_init = 0.8 - 0.6 * math.exp(-0.3 * l)
        lamv = jnp.stack([lambda_q1[l], lambda_k1[l], lambda_q2[l], lambda_k2[l]]).astype(F32)
        w_in_b = w_in[l].astype(BF16)
        wpw_b = w_conv_pw[l].astype(BF16)
        wout_b = w_out[l].astype(BF16)
        d_exp = w_exp_gate.shape[-1]
        by_group = lambda w: w.reshape(N_GROUPS, EXPERTS_PER_GROUP, d, d_exp).transpose(0, 2, 1, 3).reshape(
            N_GROUPS, d, EXPERTS_PER_GROUP * d_exp)
        wgu_b = jnp.concatenate([by_group(w_exp_gate[l]), by_group(w_exp_up[l])], axis=-1).astype(BF16)
        wd_b = w_exp_down[l].reshape(N_GROUPS, EXPERTS_PER_GROUP * d_exp, d).astype(BF16)
        gq_t = jnp.tile(g_q[l], 2 * N_HEADS).reshape(1, Q_W)
        gk_t = jnp.tile(g_k[l], 2 * N_HEADS).reshape(1, Q_W)
        gmix = g_norm_mix[l].reshape(1, d)
        gffn = g_norm_ffn[l].reshape(1, d)
        wr = jnp.zeros((d, LANE), F32)
        wr = wr.at[:, :N_GROUPS].set(w_router_group[l]).at[:, N_GROUPS:N_GROUPS + N_EXPERTS].set(w_router_expert[l])
        wr_hi = wr.astype(BF16)
        wr2 = jnp.concatenate([wr_hi, (wr - wr_hi.astype(F32)).astype(BF16)], axis=1)
        br = jnp.zeros((1, LANE), F32)
        br = br.at[0, :N_GROUPS].set(b_router_group[l]).at[0, N_GROUPS:N_GROUPS + N_EXPERTS].set(b_router_expert[l])
        conv_w = (w_dw[l], b_dw[l].reshape(1, conv_ch), g_conv_ln[l].reshape(1, conv_ch),
                  b_conv_ln[l].reshape(1, conv_ch), wpw_b, wout_b, gffn, wr2, br)

        mod = _adaln(jnp.concatenate([c_prompt, c_sample], axis=0), w_ada[l], b_ada[l])
        mod = mod.reshape(bsz + db, 6, 1, d)
        modp = [mod[:bsz, j] for j in range(6)]
        mods = [mod[bsz:, j] for j in range(6)]

        kt, v4, up, qa, ka, vt = _proj(
            yp.reshape(bsz * seq, d), modp[0], modp[1], gmix, w_in_b, gq_t, gk_t, pm,
            tm=t_att, tiles_per_seq=seq // t_att, prompt=True)
        rep = lambda m: jnp.broadcast_to(m, (db, t_new, d)).reshape(1, n_s, d)
        qsm, ksm, vsm, usm = _proj(
            ys.reshape(n_s, d), rep(mods[0]), rep(mods[1]), gmix, w_in_b, gq_t, gk_t, pm,
            tm=n_s, tiles_per_seq=1, prompt=False)
        q5 = (qsm * ATT_SCALE).reshape(db, t_new, 2 * N_HEADS, QK_DIM)
        eye = jnp.eye(2 * N_HEADS, dtype=F32)
        wq = (q5.transpose(0, 2, 1, 3)[:, :, :, None, :] * eye[None, :, None, :, None])
        wq = wq.reshape(db, rows_q, Q_W)
        paged_args = (page_table, wq, ksm.reshape(db, t_new, Q_W), vsm.reshape(db, t_new, ATT_W), slope_col,
                      qpos_col, lamv, g_attn_out[l].reshape(1, V_DIM), cache_kt, cache_v4)

        attn_args = (qa.reshape(bsz, seq, 2 * Q_W), ka.reshape(bsz, seq, 2 * Q_W), vt, lamv,
                     g_attn_out[l].reshape(V_DIM, 1))
        cp_fused = math.gcd(n_pages, 8)
        if db * (n_pages // cp_fused) <= 2 * _page_slots(bsz, seq, t_att):
            att_p, att_s = _prompt_attention(*attn_args, t=t_att, lam_init=lam_init,
                                             paged_args=paged_args, cp=cp_fused, layer=l)
        else:
            att_p, _ = _prompt_attention(*attn_args, t=t_att, lam_init=lam_init)
            att_s = _sample_attention(*paged_args, cp=math.gcd(n_pages, 16), layer=l, lam_init=lam_init)

        up3 = up.reshape(bsz, seq, conv_ch)
        x1p, h2p, lgp = _post(yp, att_p, up3, up3, modp[2], modp[3], modp[4], *conv_w,
                              tm=tm_post, seq_rows=None)
        yp = _moe(x1p, h2p, lgp, modp[5], wgu_b, wd_b, tm=tm_moe)

        us3 = usm.reshape(db, t_new, conv_ch)
        state_pad = jnp.pad(state_conv[l], ((0, 0), (HALO - (CONV_K - 1), 0), (0, 0)))
        x1s, h2s, lgs = _post(ys.reshape(1, n_s, d), att_s.reshape(1, n_s, ATT_W), us3, state_pad,
                              rep(mods[2]), rep(mods[3]), rep(mods[4]), *conv_w, tm=n_s, seq_rows=t_new)
        ys = _moe(x1s, h2s, lgs, rep(mods[5]), wgu_b, wd_b, tm=n_s).reshape(db, t_new, d)

        kp_l.append(jnp.transpose(kt.reshape(bsz, N_HEADS, 2, QK_DIM, seq), (0, 4, 1, 2, 3)))
        vp_l.append(v4.reshape(bsz, seq, N_HEADS, V_DIM))
        cp_l.append(up3[:, seq - (CONV_K - 1):])
        ks_l.append(ksm.reshape(db, t_new, N_HEADS, 2, QK_DIM))
        vs_l.append(vsm.reshape(db, t_new, N_HEADS, V_DIM))
        cs_l.append(jnp.concatenate([state_conv[l], us3], axis=1)[:, t_new:])

    return (yp, ys, jnp.stack(kp_l), jnp.stack(vp_l), jnp.stack(cp_l),
            jnp.stack(ks_l), jnp.stack(vs_l), jnp.stack(cs_l))
```
